```python
import jax, jax.numpy as jnp
from jax import lax
import numpy as np

D_MODEL = 1024
BATCH = 2
SEQ = 16384
DEPTH = 2

NSA_HEADS = 8
NSA_HEAD_DIM = 64
NSA_KV_HEADS = 2
NSA_GROUP = NSA_HEADS // NSA_KV_HEADS
NSA_WIDTH = NSA_HEADS * NSA_HEAD_DIM
KV_WIDTH = NSA_KV_HEADS * NSA_HEAD_DIM
CMP_LEN = 32
CMP_STRIDE = 16
CMP_HIDDEN = 256
SLC_LEN = 64
SLC_TOPN = 16
WINDOW = 512
Q_BLOCK = 128
RET_HEADS = 4
RET_HEAD_DIM = 128
RET_WIDTH = RET_HEADS * RET_HEAD_DIM
RET_CHUNK = 128
MEM_LEN = 256
MEM_HEADS = 4
MEM_HEAD_DIM = D_MODEL // MEM_HEADS
D_FF = 2816
EPS = 1e-6
NEG = -1e30

IN_SIZES = (NSA_WIDTH,
            KV_WIDTH, KV_WIDTH,
            KV_WIDTH, KV_WIDTH,
            KV_WIDTH, KV_WIDTH,
            NSA_HEADS * 3,
            RET_WIDTH, RET_WIDTH, RET_WIDTH, RET_WIDTH)
IN_WIDTH = sum(IN_SIZES)
IN_OFFSETS = tuple(int(o) for o in np.cumsum(IN_SIZES)[:-1])

kernel_name = "hymba_nsa_retention_macaron"


def rms_norm(x, g):
    xf = x.astype(jnp.float32)
    y = xf * lax.rsqrt(jnp.mean(xf * xf, axis=-1, keepdims=True) + EPS)
    return (y * g.astype(jnp.float32)).astype(x.dtype)


def swiglu(x, w_gate, w_up, w_down):
    return (jax.nn.silu(x @ w_gate) * (x @ w_up)) @ w_down


def masked_softmax(s, mask):
    s = jnp.where(mask, s, NEG)
    m = jnp.max(s, axis=-1, keepdims=True)
    p = jnp.where(mask, jnp.exp(s - m), 0.0)
    return p / jnp.maximum(jnp.sum(p, axis=-1, keepdims=True), 1e-30)


def alibi_slopes(n):
    return jnp.asarray(2.0 ** (-8.0 * np.arange(1, n + 1) / n), dtype=jnp.float32)


def compress(kv, pe, w1, w2):
    B, T, G, dh = kv.shape
    nc = (T - CMP_LEN) // CMP_STRIDE + 1
    idx = np.arange(nc)[:, None] * CMP_STRIDE + np.arange(CMP_LEN)[None, :]
    blocks = kv[:, idx] + pe[None, None, :, None, :]
    flat = blocks.transpose(0, 1, 3, 2, 4).reshape(B, nc, G, CMP_LEN * dh)
    return jax.nn.silu(flat @ w1) @ w2


def nsa_mixer(q, k_cmp, v_cmp, k_slc, v_slc, k_win, v_win, gates, cmp_pe, cmp_w1, cmp_w2):
    B, T = q.shape[:2]
    G, R, dh = NSA_KV_HEADS, NSA_GROUP, NSA_HEAD_DIM
    scale = dh ** -0.5
    slopes = alibi_slopes(NSA_HEADS).reshape(G, R)

    kc = compress(k_cmp, cmp_pe[0], cmp_w1[0], cmp_w2[0])
    vc = compress(v_cmp, cmp_pe[1], cmp_w1[1], cmp_w2[1])
    nc = kc.shape[1]
    nsel = T // SLC_LEN
    n_top = min(SLC_TOPN, nsel)

    c_start_np = np.arange(nc) * CMP_STRIDE
    c_end_np = c_start_np + CMP_LEN - 1
    s_start_np = np.arange(nsel) * SLC_LEN
    s_end_np = s_start_np + SLC_LEN - 1
    overlap = jnp.asarray(np.clip(np.minimum(c_end_np[:, None], s_end_np[None, :])
                                  - np.maximum(c_start_np[:, None], s_start_np[None, :]) + 1,
                                  0, None) / CMP_LEN, dtype=jnp.float32)
    c_end = jnp.asarray(c_end_np, dtype=jnp.int32)
    c_mid = jnp.asarray(c_start_np + (CMP_LEN - 1) / 2.0, dtype=jnp.float32)

    k_blocks = k_slc.reshape(B, nsel, SLC_LEN, G, dh).transpose(0, 3, 1, 2, 4)
    v_blocks = v_slc.reshape(B, nsel, SLC_LEN, G, dh).transpose(0, 3, 1, 2, 4)
    k_win_p = jnp.pad(k_win, ((0, 0), (WINDOW, 0), (0, 0), (0, 0)))
    v_win_p = jnp.pad(v_win, ((0, 0), (WINDOW, 0), (0, 0), (0, 0)))

    nb = T // Q_BLOCK
    qb = q.reshape(B, nb, Q_BLOCK, G, R, dh).transpose(1, 0, 2, 3, 4, 5)
    gb = gates.reshape(B, nb, Q_BLOCK, G, R, 3).transpose(1, 0, 2, 3, 4, 5)
    b_ix = jnp.arange(B)[:, None, None, None]
    g_ix = jnp.arange(G)[None, None, :, None]
    jb = jnp.arange(nsel)

    def block_fn(args):
        i, qi, gi = args
        t = i * Q_BLOCK + jnp.arange(Q_BLOCK)
        tf = t.astype(jnp.float32)

        s_c = jnp.einsum('bqgrd,bngd->bgrqn', qi, kc).astype(jnp.float32) * scale
        s_c = s_c - slopes[None, :, :, None, None] * (tf[:, None] - c_mid[None, :])
        p_c = masked_softmax(s_c, (c_end[None, :] <= t[:, None])[None, None, None])
        o_c = jnp.einsum('bgrqn,bngd->bqgrd', p_c.astype(vc.dtype), vc)

        imp = jnp.einsum('bgrqn,nj->bqgj', p_c, overlap)
        cur = t // SLC_LEN
        forced = (jb[None, :] == 0) | (jb[None, :] == cur[:, None]) | (jb[None, :] == cur[:, None] - 1)
        valid = jb[None, :] <= cur[:, None]
        imp = jnp.where(forced[None, :, None, :], 1e4,
                        jnp.where(valid[None, :, None, :], imp, -1.0))
        _, idx = lax.top_k(imp, n_top)

        ks = k_blocks[b_ix, g_ix, idx].reshape(B, Q_BLOCK, G, n_top * SLC_LEN, dh)
        vs = v_blocks[b_ix, g_ix, idx].reshape(B, Q_BLOCK, G, n_top * SLC_LEN, dh)
        spos = (idx[..., None] * SLC_LEN + jnp.arange(SLC_LEN)).reshape(B, Q_BLOCK, G, 1, n_top * SLC_LEN)
        dist_s = t[None, :, None, None, None] - spos
        s_s = jnp.einsum('bqgrd,bqgmd->bqgrm', qi, ks).astype(jnp.float32) * scale
        s_s = s_s - slopes[None, None, :, :, None] * dist_s.astype(jnp.float32)
        p_s = masked_softmax(s_s, dist_s >= 0)
        o_s = jnp.einsum('bqgrm,bqgmd->bqgrd', p_s.astype(vs.dtype), vs)

        kw = lax.dynamic_slice_in_dim(k_win_p, i * Q_BLOCK, WINDOW + Q_BLOCK, axis=1)
        vw = lax.dynamic_slice_in_dim(v_win_p, i * Q_BLOCK, WINDOW + Q_BLOCK, axis=1)
        kpos = i * Q_BLOCK - WINDOW + jnp.arange(WINDOW + Q_BLOCK)
        dist_w = t[:, None] - kpos[None, :]
        mask_w = (dist_w >= 0) & (dist_w < WINDOW) & (kpos[None, :] >= 0)
        s_w = jnp.einsum('bqgrd,bkgd->bqgrk', qi, kw).astype(jnp.float32) * scale
        s_w = s_w - slopes[None, None, :, :, None] * dist_w[None, :, None, None, :].astype(jnp.float32)
        p_w = masked_softmax(s_w, mask_w[None, :, None, None, :])
        o_w = jnp.einsum('bqgrk,bkgd->bqgrd', p_w.astype(vw.dtype), vw)

        gs = jax.nn.sigmoid(gi.astype(jnp.float32))
        o = (gs[..., 0:1] * o_c.astype(jnp.float32) + gs[..., 1:2] * o_s.astype(jnp.float32)
             + gs[..., 2:3] * o_w.astype(jnp.float32))
        return o.reshape(B, Q_BLOCK, NSA_WIDTH).astype(q.dtype)

    out = lax.map(block_fn, (jnp.arange(nb), qb, gb))
    return out.transpose(1, 0, 2, 3).reshape(B, T, NSA_WIDTH)


def retention(q, k, v, g, gn_gain):
    B, T, Hr, d = q.shape
    C = RET_CHUNK
    nch = T // C
    log_g = jnp.log(1.0 - jnp.exp2(-5.0 - jnp.arange(Hr, dtype=jnp.float32)))
    pos = jnp.arange(C, dtype=jnp.float32)
    diff = pos[:, None] - pos[None, :]
    decay_mask = jnp.where(diff >= 0, jnp.exp(jnp.maximum(diff, 0.0) * log_g[:, None, None]), 0.0)
    q_dec = jnp.exp((pos + 1.0) * log_g[:, None])
    k_dec = jnp.exp((C - 1.0 - pos) * log_g[:, None])
    chunk_dec = jnp.exp(C * log_g)

    def to_chunks(a):
        return a.astype(jnp.float32).reshape(B, nch, C, Hr, d).transpose(1, 0, 3, 2, 4)
    qc, kc, vc = to_chunks(q), to_chunks(k * (d ** -0.5)), to_chunks(v)

    def step(state, xs):
        qi, ki, vi = xs
        s = jnp.einsum('bhnd,bhmd->bhnm', qi, ki) * decay_mask[None]
        inner = jnp.einsum('bhnm,bhmd->bhnd', s, vi)
        cross = jnp.einsum('bhnd,bhde->bhne', qi, state) * q_dec[None, :, :, None]
        new_state = (state * chunk_dec[None, :, None, None]
                     + jnp.einsum('bhmd,bhme->bhde', ki * k_dec[None, :, :, None], vi))
        return new_state, inner + cross

    state0 = jnp.zeros((B, Hr, d, d), jnp.float32)
    _, o = lax.scan(step, state0, (qc, kc, vc))
    o = o.transpose(1, 0, 3, 2, 4).reshape(B, T, Hr, d)
    mu = jnp.mean(o, axis=-1, keepdims=True)
    var = jnp.mean(jnp.square(o - mu), axis=-1, keepdims=True)
    o = (o - mu) * lax.rsqrt(var + EPS) * gn_gain.astype(jnp.float32).reshape(Hr, d)
    o = o.reshape(B, T, RET_WIDTH) * jax.nn.silu(g.astype(jnp.float32))
    return o.astype(q.dtype)


def hybrid_mixer(h, w_in, cmp_pe, cmp_w1, cmp_w2, nsa_out_g, ret_gn_g, w_out):
    B, T, _ = h.shape
    parts = jnp.split(h @ w_in, IN_OFFSETS, axis=-1)
    q_n, kc, vc, ks, vs, kw, vw, gates, q_r, k_r, v_r, g_r = parts
    kvs = lambda a: a.reshape(B, T, NSA_KV_HEADS, NSA_HEAD_DIM)
    o_nsa = nsa_mixer(q_n.reshape(B, T, NSA_HEADS, NSA_HEAD_DIM), kvs(kc), kvs(vc), kvs(ks), kvs(vs),
                      kvs(kw), kvs(vw), gates.reshape(B, T, NSA_HEADS, 3), cmp_pe, cmp_w1, cmp_w2)
    o_nsa = rms_norm(o_nsa, nsa_out_g)
    rs = lambda a: a.reshape(B, T, RET_HEADS, RET_HEAD_DIM)
    o_ret = retention(rs(q_r), rs(k_r), rs(v_r), g_r, ret_gn_g)
    return jnp.concatenate([o_nsa, o_ret], axis=-1) @ w_out


def memory_xattn(h, m, wq, wk, wv, wo):
    B, T, _ = h.shape
    M = m.shape[1]
    q = (h @ wq).reshape(B, T, MEM_HEADS, MEM_HEAD_DIM)
    k = (m @ wk).reshape(B, M, MEM_HEADS, MEM_HEAD_DIM)
    v = (m @ wv).reshape(B, M, MEM_HEADS, MEM_HEAD_DIM)
    s = jnp.einsum('bthd,bmhd->bhtm', q, k).astype(jnp.float32) * (MEM_HEAD_DIM ** -0.5)
    p = jax.nn.softmax(s, axis=-1)
    o = jnp.einsum('bhtm,bmhd->bthd', p.astype(v.dtype), v).reshape(B, T, D_MODEL)
    return o @ wo


def setup_inputs(seed: int = 0) -> dict:
    key = jax.random.key(seed)
    ks = jax.random.split(key, 28)
    f32 = jnp.float32

    def dense(k, shape, fan_in):
        return jax.random.normal(k, shape, f32) * (fan_in ** -0.5)

    def gain(k, n):
        return 1.0 + 0.02 * jax.random.normal(k, (DEPTH, n), f32)

    L, dh = CMP_LEN, NSA_HEAD_DIM
    return {
        "x": jax.random.normal(ks[0], (BATCH, SEQ, D_MODEL), f32),
        "mem": jax.random.normal(ks[1], (BATCH, MEM_LEN, D_MODEL), f32),
        "ffn1_pre_g": gain(ks[2], D_MODEL),
        "ffn1_w_gate": dense(ks[3], (DEPTH, D_MODEL, D_FF), D_MODEL),
        "ffn1_w_up": dense(ks[4], (DEPTH, D_MODEL, D_FF), D_MODEL),
        "ffn1_w_down": dense(ks[5], (DEPTH, D_FF, D_MODEL), D_FF),
        "ffn1_post_g": gain(ks[6], D_MODEL),
        "mix_pre_g": gain(ks[7], D_MODEL),
        "w_in": dense(ks[8], (DEPTH, D_MODEL, IN_WIDTH), D_MODEL),
        "cmp_pe": 0.5 * jax.random.normal(ks[9], (DEPTH, 2, L, dh), f32),
        "cmp_w1": dense(ks[10], (DEPTH, 2, L * dh, CMP_HIDDEN), L * dh),
        "cmp_w2": dense(ks[11], (DEPTH, 2, CMP_HIDDEN, dh), CMP_HIDDEN),
        "nsa_out_g": gain(ks[12], NSA_WIDTH),
        "ret_gn_g": gain(ks[13], RET_WIDTH),
        "w_out": dense(ks[14], (DEPTH, NSA_WIDTH + RET_WIDTH, D_MODEL), NSA_WIDTH + RET_WIDTH),
        "mix_post_g": gain(ks[15], D_MODEL),
        "xa_pre_g": gain(ks[16], D_MODEL),
        "xa_mem_g": gain(ks[17], D_MODEL),
        "xa_wq": dense(ks[18], (DEPTH, D_MODEL, D_MODEL), D_MODEL),
        "xa_wk": dense(ks[19], (DEPTH, D_MODEL, D_MODEL), D_MODEL),
        "xa_wv": dense(ks[20], (DEPTH, D_MODEL, D_MODEL), D_MODEL),
        "xa_wo": dense(ks[21], (DEPTH, D_MODEL, D_MODEL), D_MODEL),
        "xa_post_g": gain(ks[22], D_MODEL),
        "ffn2_pre_g": gain(ks[23], D_MODEL),
        "ffn2_w_gate": dense(ks[24], (DEPTH, D_MODEL, D_FF), D_MODEL),
        "ffn2_w_up": dense(ks[25], (DEPTH, D_MODEL, D_FF), D_MODEL),
        "ffn2_w_down": dense(ks[26], (DEPTH, D_FF, D_MODEL), D_FF),
        "ffn2_post_g": gain(ks[27], D_MODEL),
    }


def reference(x, mem, ffn1_pre_g, ffn1_w_gate, ffn1_w_up, ffn1_w_down, ffn1_post_g,
              mix_pre_g, w_in, cmp_pe, cmp_w1, cmp_w2, nsa_out_g, ret_gn_g, w_out, mix_post_g,
              xa_pre_g, xa_mem_g, xa_wq, xa_wk, xa_wv, xa_wo, xa_post_g,
              ffn2_pre_g, ffn2_w_gate, ffn2_w_up, ffn2_w_down, ffn2_post_g):
    for l in range(DEPTH):
        y = swiglu(rms_norm(x, ffn1_pre_g[l]), ffn1_w_gate[l], ffn1_w_up[l], ffn1_w_down[l])
        x = x + 0.5 * rms_norm(y, ffn1_post_g[l])
        y = hybrid_mixer(rms_norm(x, mix_pre_g[l]), w_in[l], cmp_pe[l], cmp_w1[l], cmp_w2[l],
                         nsa_out_g[l], ret_gn_g[l], w_out[l])
        x = x + rms_norm(y, mix_post_g[l])
        y = memory_xattn(rms_norm(x, xa_pre_g[l]), rms_norm(mem, xa_mem_g[l]),
                         xa_wq[l], xa_wk[l], xa_wv[l], xa_wo[l])
        x = x + rms_norm(y, xa_post_g[l])
        y = swiglu(rms_norm(x, ffn2_pre_g[l]), ffn2_w_gate[l], ffn2_w_up[l], ffn2_w_down[l])
        x = x + 0.5 * rms_norm(y, ffn2_post_g[l])
    return x
```

```python
import functools

import numpy as np
import jax
import jax.numpy as jnp
from jax import lax
from jax.experimental import pallas as pl
from jax.experimental.pallas import tpu as pltpu

F32 = jnp.float32
BF16 = jnp.bfloat16

D_MODEL = 1024
NSA_HEADS = 8
NSA_HEAD_DIM = 64
NSA_KV_HEADS = 2
NSA_GROUP = NSA_HEADS // NSA_KV_HEADS
NSA_WIDTH = NSA_HEADS * NSA_HEAD_DIM
KV_WIDTH = NSA_KV_HEADS * NSA_HEAD_DIM
CMP_LEN = 32
CMP_STRIDE = 16
CMP_HIDDEN = 256
SLC_LEN = 64
SLC_TOPN = 16
WINDOW = 512
RET_HEADS = 4
RET_HEAD_DIM = 128
RET_WIDTH = RET_HEADS * RET_HEAD_DIM
RET_CHUNK = 128
MEM_HEADS = 4
MEM_HEAD_DIM = D_MODEL // MEM_HEADS
D_FF = 2816
EPS = 1e-6
NEG = -1e30
TAKEN = -3e38
GATE_ROWS = 32

IN_SIZES = (NSA_WIDTH, KV_WIDTH, KV_WIDTH, KV_WIDTH, KV_WIDTH, KV_WIDTH, KV_WIDTH,
            NSA_HEADS * 3, RET_WIDTH, RET_WIDTH, RET_WIDTH, RET_WIDTH)
IN_OFFSETS = tuple(int(o) for o in np.cumsum(IN_SIZES)[:-1])

LANES = 128
QPAD = NSA_HEADS * LANES
ALIBI = tuple(float(2.0 ** (-8.0 * i / NSA_HEADS)) for i in range(1, NSA_HEADS + 1))

TM = 512
TQ = 128
TK = 512
WKEYS = WINDOW + TQ
VMEM_LIMIT = 56 * 1024 * 1024

_NT = (((1,), (1,)), ((), ()))
_TN = (((0,), (0,)), ((), ()))


def _mm(a, b, preferred_element_type=F32):
    return jnp.dot(a, b, preferred_element_type=preferred_element_type)


def _mmg(a, b, dims, preferred_element_type=F32):
    return lax.dot_general(a, b, dims, preferred_element_type=preferred_element_type)


def _cparams(*sem):
    return pltpu.CompilerParams(dimension_semantics=sem, vmem_limit_bytes=VMEM_LIMIT)


def _rms(x, g):
    return x * lax.rsqrt(jnp.mean(x * x, axis=-1, keepdims=True) + EPS) * g


def _const_spec(shape):
    nd = len(shape)
    return pl.BlockSpec(shape, lambda *_: (0,) * nd, pipeline_mode=pl.Buffered(1))


def _ffn_kernel(x_ref, pre_ref, wg_ref, wu_ref, wd_ref, post_ref, o_ref):
    x = x_ref[0]
    h = _rms(x, pre_ref[...]).astype(BF16)
    g = _mm(h, wg_ref[...], preferred_element_type=F32)
    u = _mm(h, wu_ref[...], preferred_element_type=F32)
    a = (g * jax.nn.sigmoid(g) * u).astype(BF16)
    y = _mm(a, wd_ref[...], preferred_element_type=F32)
    o_ref[0] = x + 0.5 * _rms(y, post_ref[...])


def _ffn(x, pre_g, wg, wu, wd, post_g):
    B, T, D = x.shape
    tm = min(TM, T)
    return pl.pallas_call(
        _ffn_kernel,
        grid=(B, T // tm),
        in_specs=[pl.BlockSpec((1, tm, D), lambda b, i: (b, i, 0)),
                  _const_spec((1, D)), _const_spec(wg.shape), _const_spec(wu.shape),
                  _const_spec(wd.shape), _const_spec((1, D))],
        out_specs=pl.BlockSpec((1, tm, D), lambda b, i: (b, i, 0)),
        out_shape=jax.ShapeDtypeStruct(x.shape, F32),
        compiler_params=_cparams("parallel", "parallel"),
    )(x, pre_g.reshape(1, D), wg, wu, wd, post_g.reshape(1, D))


_C_Q = 0
_C_CMP = _C_Q + QPAD
_C_KS = _C_CMP + 2 * KV_WIDTH
_C_VS = _C_KS + KV_WIDTH
_C_KW = _C_VS + KV_WIDTH
_C_VW = _C_KW + KV_WIDTH
_C_GATE = _C_VW + KV_WIDTH
_C_QR = _C_GATE + LANES
_C_KR = _C_QR + RET_WIDTH
_C_VR = _C_KR + RET_WIDTH
_C_GR = _C_VR + RET_WIDTH
_C_END = _C_GR + RET_WIDTH


def _inproj_kernel(x_ref, pre_ref, w_ref, q_ref, cmp_ref, ks_ref, vst_ref, kw_ref, vwt_ref,
                   gt_ref, qr_ref, kr_ref, vr_ref, gr_ref):
    tm = x_ref.shape[1]
    h = _rms(x_ref[0], pre_ref[...]).astype(BF16)

    def proj(lo, hi):
        return _mm(h, w_ref[:, lo:hi], preferred_element_type=F32)

    q = proj(_C_Q, _C_CMP)
    for hh in range(NSA_HEADS):
        q_ref[0, hh] = q[:, hh * LANES:(hh + 1) * LANES].astype(BF16)
    c = proj(_C_CMP, _C_KS)
    cmp_ref[0, 0] = c[:, :KV_WIDTH].astype(BF16)
    cmp_ref[0, 1] = c[:, KV_WIDTH:].astype(BF16)
    ks_ref[0] = proj(_C_KS, _C_VS).astype(BF16)
    kw_ref[0] = proj(_C_KW, _C_VW).astype(BF16)
    vs = proj(_C_VS, _C_KW)
    vw = proj(_C_VW, _C_GATE)
    for j in range(tm // LANES):
        vst_ref[0, j] = vs[j * LANES:(j + 1) * LANES, :].T.astype(BF16)
        vwt_ref[0, j] = vw[j * LANES:(j + 1) * LANES, :].T.astype(BF16)
    gt_ref[0] = proj(_C_GATE, _C_QR).T[:GATE_ROWS, :]
    qr_ref[0] = proj(_C_QR, _C_KR).astype(BF16)
    kr_ref[0] = proj(_C_KR, _C_VR).astype(BF16)
    vr_ref[0] = proj(_C_VR, _C_GR).astype(BF16)
    gr_ref[0] = proj(_C_GR, _C_END)


def _inproj_weight(w_in):
    parts = jnp.split(w_in, IN_OFFSETS, axis=-1)
    q_n, kc, vc, ks, vs, kw, vw, gates, q_r, k_r, v_r, g_r = parts
    scale = NSA_HEAD_DIM ** -0.5
    qcols = []
    for hh in range(NSA_HEADS):
        g = hh // NSA_GROUP
        blk = q_n[:, hh * NSA_HEAD_DIM:(hh + 1) * NSA_HEAD_DIM] * scale
        z = jnp.zeros_like(blk)
        qcols += [blk, z] if g == 0 else [z, blk]
    gates_p = jnp.pad(gates, ((0, 0), (0, LANES - gates.shape[1])))
    w = jnp.concatenate(qcols + [kc, vc, ks, vs, kw, vw, gates_p, q_r, k_r, v_r, g_r], axis=1)
    assert w.shape[1] == _C_END
    return w.astype(BF16)


def _inproj(x, pre_g, w_all):
    B, T, D = x.shape
    tm = min(TM, T)
    nb = tm // LANES
    row = lambda w: pl.BlockSpec((1, tm, w), lambda b, i: (b, i, 0))
    out_shape = (
        jax.ShapeDtypeStruct((B, NSA_HEADS, T, LANES), BF16),
        jax.ShapeDtypeStruct((B, 2, T, KV_WIDTH), BF16),
        jax.ShapeDtypeStruct((B, T, KV_WIDTH), BF16),
        jax.ShapeDtypeStruct((B, T // LANES, KV_WIDTH, LANES), BF16),
        jax.ShapeDtypeStruct((B, T, KV_WIDTH), BF16),
        jax.ShapeDtypeStruct((B, T // LANES, KV_WIDTH, LANES), BF16),
        jax.ShapeDtypeStruct((B, GATE_ROWS, T), F32),
        jax.ShapeDtypeStruct((B, T, RET_WIDTH), BF16),
        jax.ShapeDtypeStruct((B, T, RET_WIDTH), BF16),
        jax.ShapeDtypeStruct((B, T, RET_WIDTH), BF16),
        jax.ShapeDtypeStruct((B, T, RET_WIDTH), F32),
    )
    vt_spec = pl.BlockSpec((1, nb, KV_WIDTH, LANES), lambda b, i: (b, i, 0, 0))
    out_specs = (
        pl.BlockSpec((1, NSA_HEADS, tm, LANES), lambda b, i: (b, 0, i, 0)),
        pl.BlockSpec((1, 2, tm, KV_WIDTH), lambda b, i: (b, 0, i, 0)),
        row(KV_WIDTH), vt_spec, row(KV_WIDTH), vt_spec,
        pl.BlockSpec((1, GATE_ROWS, tm), lambda b, i: (b, 0, i)),
        row(RET_WIDTH), row(RET_WIDTH), row(RET_WIDTH), row(RET_WIDTH),
    )
    return pl.pallas_call(
        _inproj_kernel,
        grid=(B, T // tm),
        in_specs=[pl.BlockSpec((1, tm, D), lambda b, i: (b, i, 0)),
                  _const_spec((1, D)), _const_spec(w_all.shape)],
        out_specs=out_specs,
        out_shape=out_shape,
        compiler_params=_cparams("parallel", "parallel"),
    )(x, pre_g.reshape(1, D), w_all)


def _compress_kernel(a_ref, pe_ref, w1_ref, w2_ref, rm_ref, t_ref):
    ncp = a_ref.shape[2]
    a = a_ref[0, 0].astype(F32)
    a_lo = (a + pe_ref[0, 0:1, :]).astype(BF16)
    a_hi = (a + pe_ref[0, 1:2, :]).astype(BF16)
    c = jnp.zeros((ncp, KV_WIDTH), F32)
    for g in range(NSA_KV_HEADS):
        p_lo = _mm(a_lo, w1_ref[0, 2 * g], preferred_element_type=F32)
        p_hi = _mm(a_hi, w1_ref[0, 2 * g + 1], preferred_element_type=F32)
        hdn = p_lo + pltpu.roll(p_hi, ncp - 1, axis=0)
        hdn = (hdn * jax.nn.sigmoid(hdn)).astype(BF16)
        c = c + _mm(hdn, w2_ref[0, g], preferred_element_type=F32)
    rows = lax.broadcasted_iota(jnp.int32, c.shape, 0)
    c = jnp.where(rows < ncp - 1, c, 0.0)
    rm_ref[0, 0] = c.astype(BF16)
    t_ref[0, 0] = c.T.astype(BF16)


def _compress_weights(cmp_pe, cmp_w1, cmp_w2):
    half = CMP_LEN // 2
    pe = cmp_pe.reshape(2, 2, half, 1, NSA_HEAD_DIM)
    pe = jnp.broadcast_to(pe, (2, 2, half, NSA_KV_HEADS, NSA_HEAD_DIM)).reshape(2, 2, half * KV_WIDTH)
    w1 = cmp_w1.reshape(2, 2, half, NSA_HEAD_DIM, CMP_HIDDEN)
    w1e = []
    for g in range(NSA_KV_HEADS):
        for hf in range(2):
            z = jnp.zeros((2, half, NSA_KV_HEADS, NSA_HEAD_DIM, CMP_HIDDEN), F32)
            z = z.at[:, :, g].set(w1[:, hf])
            w1e.append(z.reshape(2, half * KV_WIDTH, CMP_HIDDEN))
    w1e = jnp.stack(w1e, axis=1).astype(BF16)
    w2e = []
    for g in range(NSA_KV_HEADS):
        z = jnp.zeros((2, CMP_HIDDEN, NSA_KV_HEADS, NSA_HEAD_DIM), F32)
        z = z.at[:, :, g].set(cmp_w2)
        w2e.append(z.reshape(2, CMP_HIDDEN, KV_WIDTH))
    w2e = jnp.stack(w2e, axis=1).astype(BF16)
    return pe, w1e, w2e


def _compress(cmp, pe, w1e, w2e):
    B, _, T, _ = cmp.shape
    ncp = T // CMP_STRIDE
    a = cmp.reshape(B, 2, ncp, CMP_STRIDE * KV_WIDTH)
    kw = CMP_STRIDE * KV_WIDTH
    return pl.pallas_call(
        _compress_kernel,
        grid=(B, 2),
        in_specs=[pl.BlockSpec((1, 1, ncp, kw), lambda b, k: (b, k, 0, 0)),
                  pl.BlockSpec((1, 2, kw), lambda b, k: (k, 0, 0)),
                  pl.BlockSpec((1, 4, kw, CMP_HIDDEN), lambda b, k: (k, 0, 0, 0)),
                  pl.BlockSpec((1, 2, CMP_HIDDEN, KV_WIDTH), lambda b, k: (k, 0, 0, 0))],
        out_specs=(pl.BlockSpec((1, 1, ncp, KV_WIDTH), lambda b, k: (b, k, 0, 0)),
                   pl.BlockSpec((1, 1, KV_WIDTH, ncp), lambda b, k: (b, k, 0, 0))),
        out_shape=(jax.ShapeDtypeStruct((B, 2, ncp, KV_WIDTH), BF16),
                   jax.ShapeDtypeStruct((B, 2, KV_WIDTH, ncp), BF16)),
        compiler_params=_cparams("parallel", "parallel"),
    )(a, pe, w1e, w2e)


def _select_kernel(q_ref, kc_ref, vct_ref, ovl_ref, oc_ref, sel_ref, p_scr):
    tq = q_ref.shape[2]
    ncp = kc_ref.shape[2]
    nsel = ovl_ref.shape[0]
    q0 = pl.program_id(1) * tq
    q = q_ref[0].reshape(NSA_HEADS * tq, LANES)
    s_all = _mmg(kc_ref[0, 0], q, _NT, preferred_element_type=F32)

    n = lax.broadcasted_iota(jnp.int32, (ncp, tq), 0)
    t = q0 + lax.broadcasted_iota(jnp.int32, (ncp, tq), 1)
    valid = n * CMP_STRIDE + (CMP_LEN - 1) <= t
    dist = t.astype(F32) - (n.astype(F32) * CMP_STRIDE + (CMP_LEN - 1) / 2.0)

    jb = lax.broadcasted_iota(jnp.int32, (nsel, tq), 0)
    cur = (q0 + lax.broadcasted_iota(jnp.int32, (nsel, tq), 1)) // SLC_LEN
    forced = (jb == 0) | (jb == cur) | (jb == cur - 1)
    allowed = jb <= cur

    for g in range(NSA_KV_HEADS):
        psum = jnp.zeros((ncp, tq), F32)
        for r in range(NSA_GROUP):
            hh = g * NSA_GROUP + r
            s = s_all[:, hh * tq:(hh + 1) * tq] - ALIBI[hh] * dist
            s = jnp.where(valid, s, NEG)
            m = jnp.max(s, axis=0, keepdims=True)
            p = jnp.where(valid, jnp.exp(s - m), 0.0)
            den = jnp.maximum(jnp.sum(p, axis=0, keepdims=True), 1e-30)
            p = p / den
            psum = psum + p
            p_scr[:, hh * tq:(hh + 1) * tq] = p.astype(BF16)
        p_hi = psum.astype(BF16)
        p_lo = (psum - p_hi.astype(F32)).astype(BF16)
        imp = (_mm(ovl_ref[...], p_hi, preferred_element_type=F32)
               + _mm(ovl_ref[...], p_lo, preferred_element_type=F32))
        imp = jnp.where(forced, 1e4, jnp.where(allowed, imp, -1.0))

        def pick(_, carry):
            imp, sel = carry
            best = jnp.max(imp, axis=0, keepdims=True)
            first = jnp.min(jnp.where(imp == best, jb, nsel), axis=0, keepdims=True)
            hit = jb == first
            return jnp.where(hit, TAKEN, imp), jnp.where(hit, 0.0, sel)

        _, sel = lax.fori_loop(0, min(SLC_TOPN, nsel), pick,
                               (imp, jnp.full((nsel, tq), NEG, F32)))
        sel_ref[0, g] = sel

    oc = _mm(vct_ref[0, 0], p_scr[...], preferred_element_type=F32)
    for hh in range(NSA_HEADS):
        oc_ref[0, 0, hh] = oc[:, hh * tq:(hh + 1) * tq]


def _overlap_t(T):
    ncp = T // CMP_STRIDE
    nc = (T - CMP_LEN) // CMP_STRIDE + 1
    nsel = T // SLC_LEN
    c_start = np.arange(nc) * CMP_STRIDE
    c_end = c_start + CMP_LEN - 1
    s_start = np.arange(nsel) * SLC_LEN
    s_end = s_start + SLC_LEN - 1
    ov = np.clip(np.minimum(c_end[:, None], s_end[None, :])
                 - np.maximum(c_start[:, None], s_start[None, :]) + 1, 0, None) / CMP_LEN
    out = np.zeros((nsel, ncp), np.float32)
    out[:, :nc] = ov.T
    return jnp.asarray(out, dtype=BF16)


def _select(q_hm, c_rm, c_t):
    B, _, T, _ = q_hm.shape
    ncp = T // CMP_STRIDE
    nsel = T // SLC_LEN
    nqt = T // TQ
    return pl.pallas_call(
        _select_kernel,
        grid=(B, nqt),
        in_specs=[pl.BlockSpec((1, NSA_HEADS, TQ, LANES), lambda b, i: (b, 0, i, 0)),
                  pl.BlockSpec((1, 1, ncp, KV_WIDTH), lambda b, i: (b, 0, 0, 0)),
                  pl.BlockSpec((1, 1, KV_WIDTH, ncp), lambda b, i: (b, 1, 0, 0)),
                  _const_spec((nsel, ncp))],
        out_specs=(pl.BlockSpec((1, 1, NSA_HEADS, LANES, TQ), lambda b, i: (b, i, 0, 0, 0)),
                   pl.BlockSpec((1, NSA_KV_HEADS, nsel, TQ), lambda b, i: (b, 0, 0, i))),
        out_shape=(jax.ShapeDtypeStruct((B, nqt, NSA_HEADS, LANES, TQ), F32),
                   jax.ShapeDtypeStruct((B, NSA_KV_HEADS, nsel, T), F32)),
        scratch_shapes=[pltpu.VMEM((ncp, NSA_HEADS * TQ), BF16)],
        compiler_params=_cparams("parallel", "parallel"),
    )(q_hm, c_rm, c_t, _overlap_t(T))


def _attn_kernel(q_ref, ks_ref, vst_ref, kw_ref, vwt_ref, sel_ref, oc_ref, gt_ref, gain_ref,
                 o_ref, acc_scr, p_scr, m_scr, l_scr, pw_scr):
    tq = q_ref.shape[2]
    T = ks_ref.shape[1]
    qi = pl.program_id(1)
    q0 = qi * tq
    tk = min(TK, T)
    bpt = tk // SLC_LEN
    vpt = tk // LANES
    q = q_ref[0].reshape(NSA_HEADS * tq, LANES)

    c_i = lax.broadcasted_iota(jnp.int32, (tk, tq), 0)
    l_i = lax.broadcasted_iota(jnp.int32, (tk, tq), 1)
    rel = (c_i - l_i).astype(F32)

    def tile(kt, first):
        k0 = kt * tk
        s_all = _mmg(ks_ref[0, pl.ds(pl.multiple_of(k0, tk), tk), :], q, _NT,
                                preferred_element_type=F32)
        vt = jnp.concatenate([vst_ref[0, kt * vpt + j] for j in range(vpt)], axis=1)
        shift = (q0 - k0).astype(F32)
        for g in range(NSA_KV_HEADS):
            sb = sel_ref[0, g, pl.ds(pl.multiple_of(kt * bpt, bpt), bpt), :]
            for r in range(NSA_GROUP):
                hh = g * NSA_GROUP + r
                sl = slice(hh * tq, (hh + 1) * tq)
                sbh = sb - ALIBI[hh] * shift
                bias = jnp.concatenate(
                    [jnp.broadcast_to(sbh[j:j + 1, :], (SLC_LEN, tq)) for j in range(bpt)], axis=0)
                s = s_all[:, sl] + ALIBI[hh] * rel + bias
                if first:
                    s = jnp.where(c_i + (k0 - q0) <= l_i, s, NEG)
                    m_new = jnp.max(s, axis=0, keepdims=True)
                    p = jnp.exp(s - m_new)
                    l_scr[:, sl] = jnp.sum(p, axis=0, keepdims=True)
                else:
                    m_old = m_scr[:, sl]
                    m_new = jnp.maximum(m_old, jnp.max(s, axis=0, keepdims=True))
                    alpha = jnp.exp(m_old - m_new)
                    p = jnp.exp(s - m_new)
                    l_scr[:, sl] = alpha * l_scr[:, sl] + jnp.sum(p, axis=0, keepdims=True)
                    acc_scr[:, sl] = acc_scr[:, sl] * alpha
                m_scr[:, sl] = m_new
                p_scr[:, sl] = p.astype(BF16)
        pv = _mm(vt, p_scr[...], preferred_element_type=F32)
        if first:
            acc_scr[...] = pv
        else:
            acc_scr[...] += pv

    kt_diag = q0 // tk
    tile(kt_diag, True)

    def body(kt, carry):
        tile(kt, False)
        return carry

    lax.fori_loop(0, kt_diag, body, 0)

    ws = jnp.maximum(q0 - WINDOW, 0)
    wk = min(WKEYS, T)
    sw_all = _mmg(kw_ref[0, pl.ds(pl.multiple_of(ws, LANES), wk), :], q, _NT,
                             preferred_element_type=F32)
    vwt = jnp.concatenate([vwt_ref[0, ws // LANES + j] for j in range(wk // LANES)], axis=1)
    kpos = ws + lax.broadcasted_iota(jnp.int32, (wk, tq), 0)
    dist = q0 + lax.broadcasted_iota(jnp.int32, (wk, tq), 1) - kpos
    wmask = (dist >= 0) & (dist < WINDOW)
    distf = dist.astype(F32)
    lw = []
    for hh in range(NSA_HEADS):
        sl = slice(hh * tq, (hh + 1) * tq)
        s = jnp.where(wmask, sw_all[:, sl] - ALIBI[hh] * distf, NEG)
        p = jnp.exp(s - jnp.max(s, axis=0, keepdims=True))
        lw.append(jnp.sum(p, axis=0, keepdims=True))
        pw_scr[:, sl] = p.astype(BF16)
    ow = _mm(vwt, pw_scr[...], preferred_element_type=F32)

    gs = jax.nn.sigmoid(gt_ref[0])
    d_i = lax.broadcasted_iota(jnp.int32, (LANES, tq), 0)
    outs = []
    ssq = jnp.zeros((1, tq), F32)
    for hh in range(NSA_HEADS):
        g = hh // NSA_GROUP
        sl = slice(hh * tq, (hh + 1) * tq)
        o_s = acc_scr[:, sl] / jnp.maximum(l_scr[:, sl], 1e-30)
        o_w = ow[:, sl] / jnp.maximum(lw[hh], 1e-30)
        o = (gs[3 * hh:3 * hh + 1, :] * oc_ref[0, 0, hh] + gs[3 * hh + 1:3 * hh + 2, :] * o_s
             + gs[3 * hh + 2:3 * hh + 3, :] * o_w)
        o = jnp.where((d_i >= g * NSA_HEAD_DIM) & (d_i < (g + 1) * NSA_HEAD_DIM), o, 0.0)
        ssq = ssq + jnp.sum(o * o, axis=0, keepdims=True)
        outs.append(o)
    rinv = lax.rsqrt(ssq / NSA_WIDTH + EPS)
    for hh in range(NSA_HEADS):
        on = outs[hh] * rinv * gain_ref[hh]
        o_ref[0, :, hh * LANES:(hh + 1) * LANES] = on.T.astype(BF16)


def _attn_gain(nsa_out_g):
    g = nsa_out_g.reshape(NSA_HEADS, NSA_HEAD_DIM)
    pad = []
    for hh in range(NSA_HEADS):
        z = jnp.zeros((NSA_HEAD_DIM,), F32)
        pad.append(jnp.concatenate([g[hh], z] if hh // NSA_GROUP == 0 else [z, g[hh]]))
    g = jnp.stack(pad)
    return jnp.broadcast_to(g[:, :, None], (NSA_HEADS, LANES, TQ))


def _attn(q_hm, ks, vst, kw, vwt, sel, oc, gt, gain_b):
    B, _, T, _ = q_hm.shape
    nsel = T // SLC_LEN
    nqt = T // TQ
    tk = min(TK, T)
    wk = min(WKEYS, T)
    full_k = pl.BlockSpec((1, T, KV_WIDTH), lambda b, i: (b, 0, 0), pipeline_mode=pl.Buffered(1))
    full_vt = pl.BlockSpec((1, T // LANES, KV_WIDTH, LANES), lambda b, i: (b, 0, 0, 0),
                           pipeline_mode=pl.Buffered(1))
    return pl.pallas_call(
        _attn_kernel,
        grid=(B, nqt),
        in_specs=[pl.BlockSpec((1, NSA_HEADS, TQ, LANES), lambda b, i: (b, 0, i, 0)),
                  full_k, full_vt, full_k, full_vt,
                  pl.BlockSpec((1, NSA_KV_HEADS, nsel, TQ), lambda b, i: (b, 0, 0, i)),
                  pl.BlockSpec((1, 1, NSA_HEADS, LANES, TQ), lambda b, i: (b, i, 0, 0, 0)),
                  pl.BlockSpec((1, GATE_ROWS, TQ), lambda b, i: (b, 0, i)),
                  _const_spec((NSA_HEADS, LANES, TQ))],
        out_specs=pl.BlockSpec((1, TQ, QPAD), lambda b, i: (b, i, 0)),
        out_shape=jax.ShapeDtypeStruct((B, T, QPAD), BF16),
        scratch_shapes=[pltpu.VMEM((LANES, NSA_HEADS * TQ), F32),
                        pltpu.VMEM((tk, NSA_HEADS * TQ), BF16),
                        pltpu.VMEM((1, NSA_HEADS * TQ), F32),
                        pltpu.VMEM((1, NSA_HEADS * TQ), F32),
                        pltpu.VMEM((wk, NSA_HEADS * TQ), BF16)],
        compiler_params=_cparams("parallel", "arbitrary"),
    )(q_hm, ks, vst, kw, vwt, sel, oc, gt, gain_b)


def _retention_kernel(q_ref, k_ref, v_ref, g_ref, dm_ref, qd_ref, kd_ref, gn_ref, o_ref, st_scr,
                      *, chunk_dec):
    rt = q_ref.shape[1]
    C, d = RET_CHUNK, RET_HEAD_DIM

    @pl.when(pl.program_id(1) == 0)
    def _():
        st_scr[...] = jnp.zeros_like(st_scr)

    for c in range(rt // C):
        rows = slice(c * C, (c + 1) * C)
        for h in range(RET_HEADS):
            cols = slice(h * d, (h + 1) * d)
            q = q_ref[0, rows, cols]
            k = k_ref[0, rows, cols]
            v = v_ref[0, rows, cols]
            state = st_scr[h]
            s = _mmg(q, k, _NT, preferred_element_type=F32) * dm_ref[h]
            inner = _mm(s.astype(BF16), v, preferred_element_type=F32)
            cross = _mm(q, state.astype(BF16), preferred_element_type=F32) * qd_ref[h]
            kd = (k.astype(F32) * kd_ref[h]).astype(BF16)
            st_scr[h] = state * chunk_dec[h] + _mmg(kd, v, _TN, preferred_element_type=F32)
            o = inner + cross
            mu = jnp.mean(o, axis=-1, keepdims=True)
            oc = o - mu
            var = jnp.mean(oc * oc, axis=-1, keepdims=True)
            o = oc * lax.rsqrt(var + EPS) * gn_ref[:, cols]
            gg = g_ref[0, rows, cols]
            o_ref[0, rows, cols] = (o * (gg * jax.nn.sigmoid(gg))).astype(BF16)


def _retention_consts():
    C, d = RET_CHUNK, RET_HEAD_DIM
    log_g = np.log(1.0 - np.exp2(-5.0 - np.arange(RET_HEADS, dtype=np.float64)))
    pos = np.arange(C, dtype=np.float64)
    diff = pos[:, None] - pos[None, :]
    scale = d ** -0.5
    dmask = np.where(diff >= 0, np.exp(np.maximum(diff, 0.0) * log_g[:, None, None]), 0.0) * scale
    q_dec = np.exp((pos + 1.0) * log_g[:, None])
    k_dec = np.exp((C - 1.0 - pos) * log_g[:, None]) * scale
    chunk_dec = tuple(float(x) for x in np.exp(C * log_g))
    bc = lambda a: jnp.asarray(np.broadcast_to(a[:, :, None], (RET_HEADS, C, d)).astype(np.float32))
    return jnp.asarray(dmask.astype(np.float32)), bc(q_dec), bc(k_dec), chunk_dec


def _retention(qr, kr, vr, gr, gn_gain):
    B, T, W = qr.shape
    rt = min(TM, T)
    dmask, qd, kd, chunk_dec = _retention_consts()
    row = pl.BlockSpec((1, rt, W), lambda b, i: (b, i, 0))
    cst = _const_spec((RET_HEADS, RET_CHUNK, RET_HEAD_DIM))
    return pl.pallas_call(
        functools.partial(_retention_kernel, chunk_dec=chunk_dec),
        grid=(B, T // rt),
        in_specs=[row, row, row, row, cst, cst, cst, _const_spec((1, W))],
        out_specs=row,
        out_shape=jax.ShapeDtypeStruct((B, T, W), BF16),
        scratch_shapes=[pltpu.VMEM((RET_HEADS, RET_HEAD_DIM, RET_HEAD_DIM), F32)],
        compiler_params=_cparams("parallel", "arbitrary"),
    )(qr, kr, vr, gr, dmask, qd, kd, gn_gain.reshape(1, W))


def _outproj_kernel(x_ref, on_ref, or_ref, wn_ref, wr_ref, post_ref, o_ref):
    y = (_mm(on_ref[0], wn_ref[...], preferred_element_type=F32)
         + _mm(or_ref[0], wr_ref[...], preferred_element_type=F32))
    o_ref[0] = x_ref[0] + _rms(y, post_ref[...])


def _outproj_weights(w_out):
    wn = w_out[:NSA_WIDTH].reshape(NSA_HEADS, NSA_HEAD_DIM, D_MODEL)
    rows = []
    for hh in range(NSA_HEADS):
        z = jnp.zeros((NSA_HEAD_DIM, D_MODEL), F32)
        rows += [wn[hh], z] if hh // NSA_GROUP == 0 else [z, wn[hh]]
    return jnp.concatenate(rows, axis=0).astype(BF16), w_out[NSA_WIDTH:].astype(BF16)


def _outproj(x, o_nsa, o_ret, wn, wr, post_g):
    B, T, D = x.shape
    tm = min(TM, T)
    row = lambda w: pl.BlockSpec((1, tm, w), lambda b, i: (b, i, 0))
    return pl.pallas_call(
        _outproj_kernel,
        grid=(B, T // tm),
        in_specs=[row(D), row(QPAD), row(RET_WIDTH), _const_spec(wn.shape), _const_spec(wr.shape),
                  _const_spec((1, D))],
        out_specs=row(D),
        out_shape=jax.ShapeDtypeStruct(x.shape, F32),
        compiler_params=_cparams("parallel", "parallel"),
    )(x, o_nsa, o_ret, wn, wr, post_g.reshape(1, D))


def _memkv_kernel(m_ref, g_ref, wk_ref, wv_ref, k_ref, v_ref):
    h = _rms(m_ref[0], g_ref[...]).astype(BF16)
    k_ref[0] = _mm(h, wk_ref[...], preferred_element_type=F32).astype(BF16)
    v_ref[0] = _mm(h, wv_ref[...], preferred_element_type=F32).astype(BF16)


def _memkv(mem, g, wk, wv):
    B, M, D = mem.shape
    blk = pl.BlockSpec((1, M, D), lambda b: (b, 0, 0))
    return pl.pallas_call(
        _memkv_kernel,
        grid=(B,),
        in_specs=[blk, _const_spec((1, D)), _const_spec(wk.shape), _const_spec(wv.shape)],
        out_specs=(blk, blk),
        out_shape=(jax.ShapeDtypeStruct(mem.shape, BF16), jax.ShapeDtypeStruct(mem.shape, BF16)),
        compiler_params=_cparams("parallel"),
    )(mem, g.reshape(1, D), wk, wv)


def _xattn_kernel(x_ref, pre_ref, wq_ref, k_ref, v_ref, wo_ref, post_ref, o_ref):
    x = x_ref[0]
    h = _rms(x, pre_ref[...]).astype(BF16)
    q = (_mm(h, wq_ref[...], preferred_element_type=F32) * (MEM_HEAD_DIM ** -0.5)).astype(BF16)
    heads = []
    for hd in range(MEM_HEADS):
        cols = slice(hd * MEM_HEAD_DIM, (hd + 1) * MEM_HEAD_DIM)
        s = _mmg(q[:, cols], k_ref[0, :, cols], _NT, preferred_element_type=F32)
        p = jnp.exp(s - jnp.max(s, axis=-1, keepdims=True))
        den = jnp.sum(p, axis=-1, keepdims=True)
        oh = _mm(p.astype(BF16), v_ref[0, :, cols], preferred_element_type=F32) / den
        heads.append(oh.astype(BF16))
    o = jnp.concatenate(heads, axis=1)
    y = _mm(o, wo_ref[...], preferred_element_type=F32)
    o_ref[0] = x + _rms(y, post_ref[...])


def _xattn(x, pre_g, wq, km, vm, wo, post_g):
    B, T, D = x.shape
    M = km.shape[1]
    tm = min(TM, T)
    row = pl.BlockSpec((1, tm, D), lambda b, i: (b, i, 0))
    kv = pl.BlockSpec((1, M, D), lambda b, i: (b, 0, 0))
    return pl.pallas_call(
        _xattn_kernel,
        grid=(B, T // tm),
        in_specs=[row, _const_spec((1, D)), _const_spec(wq.shape), kv, kv, _const_spec(wo.shape),
                  _const_spec((1, D))],
        out_specs=row,
        out_shape=jax.ShapeDtypeStruct(x.shape, F32),
        compiler_params=_cparams("parallel", "parallel"),
    )(x, pre_g.reshape(1, D), wq, km, vm, wo, post_g.reshape(1, D))


def _hybrid_mixer(x, pre_g, w_in, cmp_pe, cmp_w1, cmp_w2, nsa_out_g, ret_gn_g, w_out, post_g):
    q_hm, cmp, ks, vst, kw, vwt, gt, qr, kr, vr, gr = _inproj(x, pre_g, _inproj_weight(w_in))
    c_rm, c_t = _compress(cmp, *_compress_weights(cmp_pe, cmp_w1, cmp_w2))
    oc, sel = _select(q_hm, c_rm, c_t)
    o_nsa = _attn(q_hm, ks, vst, kw, vwt, sel, oc, gt, _attn_gain(nsa_out_g))
    o_ret = _retention(qr, kr, vr, gr, ret_gn_g)
    wn, wr = _outproj_weights(w_out)
    return _outproj(x, o_nsa, o_ret, wn, wr, post_g)


def kernel(x, mem, ffn1_pre_g, ffn1_w_gate, ffn1_w_up, ffn1_w_down, ffn1_post_g, mix_pre_g, w_in, cmp_pe, cmp_w1, cmp_w2, nsa_out_g, ret_gn_g, w_out, mix_post_g, xa_pre_g, xa_mem_g, xa_wq, xa_wk, xa_wv, xa_wo, xa_post_g, ffn2_pre_g, ffn2_w_gate, ffn2_w_up, ffn2_w_down, ffn2_post_g):
    depth = w_in.shape[0]
    bf = lambda w: w.astype(BF16)
    for l in range(depth):
        x = _ffn(x, ffn1_pre_g[l], bf(ffn1_w_gate[l]), bf(ffn1_w_up[l]), bf(ffn1_w_down[l]),
                 ffn1_post_g[l])
        x = _hybrid_mixer(x, mix_pre_g[l], w_in[l], cmp_pe[l], cmp_w1[l], cmp_w2[l], nsa_out_g[l],
                          ret_gn_g[l], w_out[l], mix_post_g[l])
        km, vm = _memkv(mem, xa_mem_g[l], bf(xa_wk[l]), bf(xa_wv[l]))
        x = _xattn(x, xa_pre_g[l], bf(xa_wq[l]), km, vm, bf(xa_wo[l]), xa_post_g[l])
        x = _ffn(x, ffn2_pre_g[l], bf(ffn2_w_gate[l]), bf(ffn2_w_up[l]), bf(ffn2_w_down[l]),
                 ffn2_post_g[l])
    return x
```

```python
import functools

import numpy as np
import jax
import jax.numpy as jnp
from jax import lax
from jax.experimental import pallas as pl
from jax.experimental.pallas import tpu as pltpu

F32 = jnp.float32
BF16 = jnp.bfloat16

D_MODEL = 1024
NSA_HEADS = 8
NSA_HEAD_DIM = 64
NSA_KV_HEADS = 2
NSA_GROUP = NSA_HEADS // NSA_KV_HEADS
NSA_WIDTH = NSA_HEADS * NSA_HEAD_DIM
KV_WIDTH = NSA_KV_HEADS * NSA_HEAD_DIM
CMP_LEN = 32
CMP_STRIDE = 16
CMP_HIDDEN = 256
SLC_LEN = 64
SLC_TOPN = 16
WINDOW = 512
RET_HEADS = 4
RET_HEAD_DIM = 128
RET_WIDTH = RET_HEADS * RET_HEAD_DIM
RET_CHUNK = 128
MEM_HEADS = 4
MEM_HEAD_DIM = D_MODEL // MEM_HEADS
D_FF = 2816
EPS = 1e-6
NEG = -1e30
TAKEN = -3e38
GATE_ROWS = 32

IN_SIZES = (NSA_WIDTH, KV_WIDTH, KV_WIDTH, KV_WIDTH, KV_WIDTH, KV_WIDTH, KV_WIDTH,
            NSA_HEADS * 3, RET_WIDTH, RET_WIDTH, RET_WIDTH, RET_WIDTH)
IN_OFFSETS = tuple(int(o) for o in np.cumsum(IN_SIZES)[:-1])

LANES = 128
QPAD = NSA_HEADS * LANES
LOG2E = 1.4426950408889634
ALIBI2 = tuple(float(2.0 ** (-8.0 * i / NSA_HEADS)) * LOG2E for i in range(1, NSA_HEADS + 1))
ONES_ROWS = 16
VT_ROWS = KV_WIDTH + ONES_ROWS
CH = 256

TM = 512
TQ = 128
TK = 512
WKEYS = WINDOW + TQ
VMEM_LIMIT = 56 * 1024 * 1024

_NT = (((1,), (1,)), ((), ()))
_TN = (((0,), (0,)), ((), ()))


def _mm(a, b, preferred_element_type=F32):
    return jnp.dot(a, b, preferred_element_type=preferred_element_type)


def _mmg(a, b, dims, preferred_element_type=F32):
    return lax.dot_general(a, b, dims, preferred_element_type=preferred_element_type)


def _cparams(*sem):
    return pltpu.CompilerParams(dimension_semantics=sem, vmem_limit_bytes=VMEM_LIMIT)


def _rms(x, g):
    return x * lax.rsqrt(jnp.mean(x * x, axis=-1, keepdims=True) + EPS) * g


def _const_spec(shape):
    nd = len(shape)
    return pl.BlockSpec(shape, lambda *_: (0,) * nd, pipeline_mode=pl.Buffered(1))


def _ffn_kernel(x_ref, pre_ref, wg_ref, wu_ref, wd_ref, post_ref, o_ref):
    x = x_ref[0]
    h = _rms(x, pre_ref[...]).astype(BF16)
    g = _mm(h, wg_ref[...], preferred_element_type=F32)
    u = _mm(h, wu_ref[...], preferred_element_type=F32)
    a = (g * jax.nn.sigmoid(g) * u).astype(BF16)
    y = _mm(a, wd_ref[...], preferred_element_type=F32)
    o_ref[0] = x + 0.5 * _rms(y, post_ref[...])


def _ffn(x, pre_g, wg, wu, wd, post_g):
    B, T, D = x.shape
    tm = min(TM, T)
    return pl.pallas_call(
        _ffn_kernel,
        grid=(B, T // tm),
        in_specs=[pl.BlockSpec((1, tm, D), lambda b, i: (b, i, 0)),
                  _const_spec((1, D)), _const_spec(wg.shape), _const_spec(wu.shape),
                  _const_spec(wd.shape), _const_spec((1, D))],
        out_specs=pl.BlockSpec((1, tm, D), lambda b, i: (b, i, 0)),
        out_shape=jax.ShapeDtypeStruct(x.shape, F32),
        compiler_params=_cparams("parallel", "parallel"),
    )(x, pre_g.reshape(1, D), wg, wu, wd, post_g.reshape(1, D))


_C_Q = 0
_C_CMP = _C_Q + QPAD
_C_KS = _C_CMP + 2 * KV_WIDTH
_C_VS = _C_KS + KV_WIDTH
_C_KW = _C_VS + KV_WIDTH
_C_VW = _C_KW + KV_WIDTH
_C_GATE = _C_VW + KV_WIDTH
_C_QR = _C_GATE + LANES
_C_KR = _C_QR + RET_WIDTH
_C_VR = _C_KR + RET_WIDTH
_C_GR = _C_VR + RET_WIDTH
_C_END = _C_GR + RET_WIDTH


def _inproj_kernel(x_ref, pre_ref, w_ref, q_ref, cmp_ref, ks_ref, vst_ref, kw_ref, vwt_ref,
                   gt_ref, qr_ref, kr_ref, vr_ref, gr_ref):
    tm = x_ref.shape[1]
    h = _rms(x_ref[0], pre_ref[...]).astype(BF16)

    def proj(lo, hi):
        return _mm(h, w_ref[:, lo:hi], preferred_element_type=F32)

    q = proj(_C_Q, _C_CMP)
    for hh in range(NSA_HEADS):
        q_ref[0, hh] = q[:, hh * LANES:(hh + 1) * LANES].astype(BF16)
    c = proj(_C_CMP, _C_KS)
    cmp_ref[0, 0] = c[:, :KV_WIDTH].astype(BF16)
    cmp_ref[0, 1] = c[:, KV_WIDTH:].astype(BF16)
    ks_ref[0] = proj(_C_KS, _C_VS).astype(BF16)
    kw_ref[0] = proj(_C_KW, _C_VW).astype(BF16)
    vs = proj(_C_VS, _C_KW)
    vw = proj(_C_VW, _C_GATE)
    ones = jnp.ones((ONES_ROWS, LANES), BF16)
    for j in range(tm // LANES):
        vst_ref[0, j, :KV_WIDTH, :] = vs[j * LANES:(j + 1) * LANES, :].T.astype(BF16)
        vwt_ref[0, j, :KV_WIDTH, :] = vw[j * LANES:(j + 1) * LANES, :].T.astype(BF16)
        vst_ref[0, j, KV_WIDTH:, :] = ones
        vwt_ref[0, j, KV_WIDTH:, :] = ones
    gt_ref[0] = proj(_C_GATE, _C_QR).T[:GATE_ROWS, :]
    qr_ref[0] = proj(_C_QR, _C_KR).astype(BF16)
    kr_ref[0] = proj(_C_KR, _C_VR).astype(BF16)
    vr_ref[0] = proj(_C_VR, _C_GR).astype(BF16)
    gr_ref[0] = proj(_C_GR, _C_END)


def _inproj_weight(w_in):
    parts = jnp.split(w_in, IN_OFFSETS, axis=-1)
    q_n, kc, vc, ks, vs, kw, vw, gates, q_r, k_r, v_r, g_r = parts
    scale = NSA_HEAD_DIM ** -0.5 * LOG2E
    qcols = []
    for hh in range(NSA_HEADS):
        g = hh // NSA_GROUP
        blk = q_n[:, hh * NSA_HEAD_DIM:(hh + 1) * NSA_HEAD_DIM] * scale
        z = jnp.zeros_like(blk)
        qcols += [blk, z] if g == 0 else [z, blk]
    gates_p = jnp.pad(gates, ((0, 0), (0, LANES - gates.shape[1])))
    w = jnp.concatenate(qcols + [kc, vc, ks, vs, kw, vw, gates_p, q_r, k_r, v_r, g_r], axis=1)
    assert w.shape[1] == _C_END
    return w.astype(BF16)


def _inproj(x, pre_g, w_all):
    B, T, D = x.shape
    tm = min(TM, T)
    nb = tm // LANES
    row = lambda w: pl.BlockSpec((1, tm, w), lambda b, i: (b, i, 0))
    out_shape = (
        jax.ShapeDtypeStruct((B, NSA_HEADS, T, LANES), BF16),
        jax.ShapeDtypeStruct((B, 2, T, KV_WIDTH), BF16),
        jax.ShapeDtypeStruct((B, T, KV_WIDTH), BF16),
        jax.ShapeDtypeStruct((B, T // LANES, VT_ROWS, LANES), BF16),
        jax.ShapeDtypeStruct((B, T, KV_WIDTH), BF16),
        jax.ShapeDtypeStruct((B, T // LANES, VT_ROWS, LANES), BF16),
        jax.ShapeDtypeStruct((B, GATE_ROWS, T), F32),
        jax.ShapeDtypeStruct((B, T, RET_WIDTH), BF16),
        jax.ShapeDtypeStruct((B, T, RET_WIDTH), BF16),
        jax.ShapeDtypeStruct((B, T, RET_WIDTH), BF16),
        jax.ShapeDtypeStruct((B, T, RET_WIDTH), F32),
    )
    vt_spec = pl.BlockSpec((1, nb, VT_ROWS, LANES), lambda b, i: (b, i, 0, 0))
    out_specs = (
        pl.BlockSpec((1, NSA_HEADS, tm, LANES), lambda b, i: (b, 0, i, 0)),
        pl.BlockSpec((1, 2, tm, KV_WIDTH), lambda b, i: (b, 0, i, 0)),
        row(KV_WIDTH), vt_spec, row(KV_WIDTH), vt_spec,
        pl.BlockSpec((1, GATE_ROWS, tm), lambda b, i: (b, 0, i)),
        row(RET_WIDTH), row(RET_WIDTH), row(RET_WIDTH), row(RET_WIDTH),
    )
    return pl.pallas_call(
        _inproj_kernel,
        grid=(B, T // tm),
        in_specs=[pl.BlockSpec((1, tm, D), lambda b, i: (b, i, 0)),
                  _const_spec((1, D)), _const_spec(w_all.shape)],
        out_specs=out_specs,
        out_shape=out_shape,
        compiler_params=_cparams("parallel", "parallel"),
    )(x, pre_g.reshape(1, D), w_all)


def _compress_kernel(a_ref, pe_ref, w1_ref, w2_ref, rm_ref, t_ref, hi_scr):
    ncp = a_ref.shape[2]
    a = a_ref[0, 0].astype(F32)
    a_lo = (a + pe_ref[0, 0:1, :]).astype(BF16)
    a_hi = (a + pe_ref[0, 1:2, :]).astype(BF16)
    c = jnp.zeros((ncp, KV_WIDTH), F32)
    hi_scr[pl.ds(ncp, 8), :] = jnp.zeros((8, CMP_HIDDEN), F32)
    for g in range(NSA_KV_HEADS):
        p_lo = _mm(a_lo, w1_ref[0, 2 * g], preferred_element_type=F32)
        hi_scr[pl.ds(0, ncp), :] = _mm(a_hi, w1_ref[0, 2 * g + 1], preferred_element_type=F32)
        hdn = p_lo + hi_scr[pl.ds(1, ncp), :]
        hdn = (hdn * jax.nn.sigmoid(hdn)).astype(BF16)
        c = c + _mm(hdn, w2_ref[0, g], preferred_element_type=F32)
    rows = lax.broadcasted_iota(jnp.int32, c.shape, 0)
    c = jnp.where(rows < ncp - 1, c, 0.0)
    rm_ref[0, 0] = c.astype(BF16)
    for j in range(ncp // CH):
        t_ref[0, 0, j] = c[j * CH:(j + 1) * CH, :].T.astype(BF16)


def _compress_weights(cmp_pe, cmp_w1, cmp_w2):
    half = CMP_LEN // 2
    pe = cmp_pe.reshape(2, 2, half, 1, NSA_HEAD_DIM)
    pe = jnp.broadcast_to(pe, (2, 2, half, NSA_KV_HEADS, NSA_HEAD_DIM)).reshape(2, 2, half * KV_WIDTH)
    w1 = cmp_w1.reshape(2, 2, half, NSA_HEAD_DIM, CMP_HIDDEN)
    w1e = []
    for g in range(NSA_KV_HEADS):
        for hf in range(2):
            z = jnp.zeros((2, half, NSA_KV_HEADS, NSA_HEAD_DIM, CMP_HIDDEN), F32)
            z = z.at[:, :, g].set(w1[:, hf])
            w1e.append(z.reshape(2, half * KV_WIDTH, CMP_HIDDEN))
    w1e = jnp.stack(w1e, axis=1).astype(BF16)
    w2e = []
    for g in range(NSA_KV_HEADS):
        z = jnp.zeros((2, CMP_HIDDEN, NSA_KV_HEADS, NSA_HEAD_DIM), F32)
        z = z.at[:, :, g].set(cmp_w2)
        w2e.append(z.reshape(2, CMP_HIDDEN, KV_WIDTH))
    w2e = jnp.stack(w2e, axis=1).astype(BF16)
    return pe, w1e, w2e


def _compress(cmp, pe, w1e, w2e):
    B, _, T, _ = cmp.shape
    ncp = T // CMP_STRIDE
    a = cmp.reshape(B, 2, ncp, CMP_STRIDE * KV_WIDTH)
    kw = CMP_STRIDE * KV_WIDTH
    return pl.pallas_call(
        _compress_kernel,
        grid=(B, 2),
        in_specs=[pl.BlockSpec((1, 1, ncp, kw), lambda b, k: (b, k, 0, 0)),
                  pl.BlockSpec((1, 2, kw), lambda b, k: (k, 0, 0)),
                  pl.BlockSpec((1, 4, kw, CMP_HIDDEN), lambda b, k: (k, 0, 0, 0)),
                  pl.BlockSpec((1, 2, CMP_HIDDEN, KV_WIDTH), lambda b, k: (k, 0, 0, 0))],
        out_specs=(pl.BlockSpec((1, 1, ncp, KV_WIDTH), lambda b, k: (b, k, 0, 0)),
                   pl.BlockSpec((1, 1, ncp // CH, KV_WIDTH, CH), lambda b, k: (b, k, 0, 0, 0))),
        out_shape=(jax.ShapeDtypeStruct((B, 2, ncp, KV_WIDTH), BF16),
                   jax.ShapeDtypeStruct((B, 2, ncp // CH, KV_WIDTH, CH), BF16)),
        scratch_shapes=[pltpu.VMEM((ncp + 8, CMP_HIDDEN), F32)],
        compiler_params=_cparams("parallel", "parallel"),
    )(a, pe, w1e, w2e)


def _select_kernel(q_ref, kc_ref, vct_ref, ovl_ref, tab_ref, grp_ref, oc_ref, sel_ref, flag_ref,
                   s_scr, p_scr, imp_scr, oc_scr):
    tq = q_ref.shape[2]
    nsel = ovl_ref.shape[1]
    q0 = pl.program_id(1) * tq
    q = q_ref[0].reshape(NSA_HEADS * tq, LANES)
    span = CH * CMP_STRIDE
    nch = (q0 + tq - CMP_LEN) // span + 1
    nfull = jnp.maximum((q0 - (span + CMP_LEN - CMP_STRIDE - 1)) // span + 1, 0)
    mrel = (lax.broadcasted_iota(jnp.int32, (CH, tq), 0) * CMP_STRIDE + (CMP_LEN - 1)
            - lax.broadcasted_iota(jnp.int32, (CH, tq), 1))
    heads = [slice(hh * tq, (hh + 1) * tq) for hh in range(NSA_HEADS)]

    def chunk_rows(c):
        return pl.ds(pl.multiple_of(c * CH, CH), CH)

    def scores(masked):
        def body(c, ms):
            rows = chunk_rows(c)
            s_c = _mmg(kc_ref[0, 0, rows, :], q, _NT, preferred_element_type=F32)
            ok = mrel <= q0 - c * span
            out = []
            for hh, sl in enumerate(heads):
                s = s_c[:, sl] + tab_ref[hh, rows, :]
                if masked:
                    s = jnp.where(ok, s, NEG)
                s_scr[rows, sl] = s
                out.append(jnp.maximum(ms[hh], jnp.max(s, axis=0, keepdims=True)))
            return tuple(out)
        return body

    m0 = tuple(jnp.full((1, tq), NEG, F32) for _ in heads)
    ms = lax.fori_loop(0, nfull, scores(False), m0)
    ms = lax.fori_loop(nfull, nch, scores(True), ms)

    def probs(masked):
        def body(c, dens):
            rows = chunk_rows(c)
            ok = mrel <= q0 - c * span
            out = []
            for hh, sl in enumerate(heads):
                p = jnp.exp2(s_scr[rows, sl] - ms[hh])
                if masked:
                    p = jnp.where(ok, p, 0.0)
                s_scr[rows, sl] = p
                out.append(dens[hh] + jnp.sum(p, axis=0, keepdims=True))
            return tuple(out)
        return body

    d0 = tuple(jnp.zeros((1, tq), F32) for _ in heads)
    dens = lax.fori_loop(0, nfull, probs(False), d0)
    dens = lax.fori_loop(nfull, nch, probs(True), dens)
    rden = [1.0 / jnp.maximum(d, 1e-30) for d in dens]

    imp_scr[...] = jnp.zeros_like(imp_scr)
    oc_scr[...] = jnp.zeros_like(oc_scr)

    def outputs(c, carry):
        rows = chunk_rows(c)
        for g in range(NSA_KV_HEADS):
            psum = jnp.zeros((CH, tq), F32)
            for r in range(NSA_GROUP):
                hh = g * NSA_GROUP + r
                pn = s_scr[rows, heads[hh]] * rden[hh]
                psum = psum + pn
                p_scr[:, heads[hh]] = pn.astype(BF16)
            p_hi = psum.astype(BF16)
            p_lo = (psum - p_hi.astype(F32)).astype(BF16)
            imp_scr[g] += (_mm(ovl_ref[c], p_hi, preferred_element_type=F32)
                           + _mm(ovl_ref[c], p_lo, preferred_element_type=F32))
        oc_scr[...] += _mm(vct_ref[0, 0, c], p_scr[...], preferred_element_type=F32)
        return carry

    lax.fori_loop(0, nch, outputs, 0)
    for hh, sl in enumerate(heads):
        oc_ref[0, 0, hh] = oc_scr[:, sl]

    jb = lax.broadcasted_iota(jnp.int32, (nsel, tq), 0)
    cur = (q0 + lax.broadcasted_iota(jnp.int32, (nsel, tq), 1)) // SLC_LEN
    forced = (jb == 0) | (jb == cur) | (jb == cur - 1)
    allowed = jb <= cur
    imps = tuple(jnp.where(forced, 1e4, jnp.where(allowed, imp_scr[g], -1.0))
                 for g in range(NSA_KV_HEADS))

    def pick(_, imps):
        out = []
        for imp in imps:
            best = jnp.max(imp, axis=0, keepdims=True)
            first = jnp.min(jnp.where(imp == best, jb, nsel), axis=0, keepdims=True)
            out.append(jnp.where(jb == first, TAKEN, imp))
        return tuple(out)

    imps = lax.fori_loop(0, min(SLC_TOPN, nsel), pick, imps)
    ones = jnp.ones((ONES_ROWS, tq), BF16)
    cnt = jnp.zeros((ONES_ROWS, nsel), F32)
    for g in range(NSA_KV_HEADS):
        taken = imps[g] < -2.0
        sel_ref[0, g] = jnp.where(taken, 0.0, NEG)
        cnt = cnt + _mmg(ones, jnp.where(taken, 1.0, 0.0).astype(BF16), _NT,
                         preferred_element_type=F32)
    act = _mm(jnp.minimum(cnt, 1.0).astype(BF16), grp_ref[...], preferred_element_type=F32)
    flag_ref[0, 0] = (act[0:1, :] > 0.0).astype(jnp.int32)


def _overlap_t(T):
    ncp = T // CMP_STRIDE
    nc = (T - CMP_LEN) // CMP_STRIDE + 1
    nsel = T // SLC_LEN
    c_start = np.arange(nc) * CMP_STRIDE
    c_end = c_start + CMP_LEN - 1
    s_start = np.arange(nsel) * SLC_LEN
    s_end = s_start + SLC_LEN - 1
    ov = np.clip(np.minimum(c_end[:, None], s_end[None, :])
                 - np.maximum(c_start[:, None], s_start[None, :]) + 1, 0, None) / CMP_LEN
    out = np.zeros((nsel, ncp), np.float32)
    out[:, :nc] = ov.T
    out = out.reshape(nsel, ncp // CH, CH).transpose(1, 0, 2)
    return jnp.asarray(out, dtype=BF16)


def _select_tables(T):
    ncp = T // CMP_STRIDE
    nsel = T // SLC_LEN
    nkt = T // min(TK, T)
    n = np.arange(ncp, dtype=np.float64)[:, None]
    l = np.arange(TQ, dtype=np.float64)[None, :]
    dist = l - (n * CMP_STRIDE + (CMP_LEN - 1) / 2.0)
    tab = np.stack([-a * dist for a in ALIBI2]).astype(np.float32)
    grp = (np.arange(nsel)[:, None] // (nsel // nkt) == np.arange(nkt)[None, :]).astype(np.float32)
    return jnp.asarray(tab), jnp.asarray(grp, dtype=BF16)


def _select(q_hm, c_rm, c_t):
    B, _, T, _ = q_hm.shape
    ncp = T // CMP_STRIDE
    nsel = T // SLC_LEN
    nqt = T // TQ
    nkt = T // min(TK, T)
    tab, grp = _select_tables(T)
    return pl.pallas_call(
        _select_kernel,
        grid=(B, nqt),
        in_specs=[pl.BlockSpec((1, NSA_HEADS, TQ, LANES), lambda b, i: (b, 0, i, 0)),
                  pl.BlockSpec((1, 1, ncp, KV_WIDTH), lambda b, i: (b, 0, 0, 0)),
                  pl.BlockSpec((1, 1, ncp // CH, KV_WIDTH, CH), lambda b, i: (b, 1, 0, 0, 0)),
                  _const_spec((ncp // CH, nsel, CH)), _const_spec(tab.shape), _const_spec(grp.shape)],
        out_specs=(pl.BlockSpec((1, 1, NSA_HEADS, LANES, TQ), lambda b, i: (b, i, 0, 0, 0)),
                   pl.BlockSpec((1, NSA_KV_HEADS, nsel, TQ), lambda b, i: (b, 0, 0, i)),
                   pl.BlockSpec((1, 1, 1, nkt), lambda b, i: (b, i, 0, 0))),
        out_shape=(jax.ShapeDtypeStruct((B, nqt, NSA_HEADS, LANES, TQ), F32),
                   jax.ShapeDtypeStruct((B, NSA_KV_HEADS, nsel, T), F32),
                   jax.ShapeDtypeStruct((B, nqt, 1, nkt), jnp.int32)),
        scratch_shapes=[pltpu.VMEM((ncp, NSA_HEADS * TQ), F32),
                        pltpu.VMEM((CH, NSA_HEADS * TQ), BF16),
                        pltpu.VMEM((NSA_KV_HEADS, nsel, TQ), F32),
                        pltpu.VMEM((KV_WIDTH, NSA_HEADS * TQ), F32)],
        compiler_params=_cparams("parallel", "arbitrary"),
    )(q_hm, c_rm, c_t, _overlap_t(T), tab, grp)


def _attn_kernel(flags_ref, q_ref, ks_ref, vst_ref, kw_ref, vwt_ref, sel_ref, oc_ref, gt_ref,
                 gain_ref, ar_ref, wb_ref, o_ref, acc_scr, p_scr, m_scr, alpha_scr, pw_scr, ow_scr):
    tq = q_ref.shape[2]
    T = ks_ref.shape[1]
    qi = pl.program_id(1)
    q0 = qi * tq
    tk = ar_ref.shape[1]
    wk = wb_ref.shape[1]
    nkt = T // tk
    bpt = tk // SLC_LEN
    vpt = tk // LANES
    q = q_ref[0].reshape(NSA_HEADS * tq, LANES)
    heads = [slice(hh * tq, (hh + 1) * tq) for hh in range(NSA_HEADS)]

    def tile(kt, first):
        k0 = kt * tk
        s_all = _mmg(ks_ref[0, pl.ds(pl.multiple_of(k0, tk), tk), :], q, _NT,
                     preferred_element_type=F32)
        vt = jnp.concatenate([vst_ref[0, kt * vpt + j] for j in range(vpt)], axis=1)
        shift = (q0 - k0).astype(F32)
        if first:
            causal = (lax.broadcasted_iota(jnp.int32, (tk, tq), 0) + (k0 - q0)
                      <= lax.broadcasted_iota(jnp.int32, (tk, tq), 1))
        for g in range(NSA_KV_HEADS):
            sb = sel_ref[0, g, pl.ds(pl.multiple_of(kt * bpt, bpt), bpt), :]
            for r in range(NSA_GROUP):
                hh = g * NSA_GROUP + r
                sl = heads[hh]
                sbh = sb - ALIBI2[hh] * shift
                bias = jnp.concatenate(
                    [jnp.broadcast_to(sbh[j:j + 1, :], (SLC_LEN, tq)) for j in range(bpt)], axis=0)
                s = s_all[:, sl] + ar_ref[hh] + bias
                if first:
                    s = jnp.where(causal, s, NEG)
                    m_new = jnp.max(s, axis=0, keepdims=True)
                else:
                    m_old = m_scr[:, sl]
                    m_new = jnp.maximum(m_old, jnp.max(s, axis=0, keepdims=True))
                    alpha_scr[:, sl] = jnp.exp2(m_old - m_new)
                m_scr[:, sl] = m_new
                p_scr[:, sl] = jnp.exp2(s - m_new).astype(BF16)
        pv = _mm(vt, p_scr[...], preferred_element_type=F32)
        if first:
            acc_scr[...] = pv
        else:
            acc_scr[...] = acc_scr[...] * alpha_scr[...] + pv

    kt_diag = q0 // tk
    tile(kt_diag, True)
    fbase = (pl.program_id(0) * pl.num_programs(1) + qi) * nkt

    def body(kt, carry):
        @pl.when(flags_ref[fbase + kt] != 0)
        def _():
            tile(kt, False)
        return carry

    lax.fori_loop(0, kt_diag, body, 0)

    def window(ws, general):
        sw_all = _mmg(kw_ref[0, pl.ds(pl.multiple_of(ws, LANES), wk), :], q, _NT,
                      preferred_element_type=F32)
        vwt = jnp.concatenate([vwt_ref[0, ws // LANES + j] for j in range(wk // LANES)], axis=1)
        if general:
            kpos = ws + lax.broadcasted_iota(jnp.int32, (wk, tq), 0)
            dist = q0 + lax.broadcasted_iota(jnp.int32, (wk, tq), 1) - kpos
            wmask = (dist >= 0) & (dist < WINDOW)
            distf = dist.astype(F32)
        for hh, sl in enumerate(heads):
            if general:
                s = jnp.where(wmask, sw_all[:, sl] - ALIBI2[hh] * distf, NEG)
            else:
                s = sw_all[:, sl] + wb_ref[hh]
            pw_scr[:, sl] = jnp.exp2(s - jnp.max(s, axis=0, keepdims=True)).astype(BF16)
        ow_scr[...] = _mm(vwt, pw_scr[...], preferred_element_type=F32)

    @pl.when(q0 >= WINDOW)
    def _():
        window(q0 - WINDOW, False)

    @pl.when(q0 < WINDOW)
    def _():
        window(q0 * 0, True)

    gs = jax.nn.sigmoid(gt_ref[0])
    d_i = lax.broadcasted_iota(jnp.int32, (LANES, tq), 0)
    outs = []
    ssq = jnp.zeros((1, tq), F32)
    for hh in range(NSA_HEADS):
        g = hh // NSA_GROUP
        sl = heads[hh]
        o_s = acc_scr[:KV_WIDTH, sl] / jnp.maximum(acc_scr[KV_WIDTH:KV_WIDTH + 1, sl], 1e-30)
        o_w = ow_scr[:KV_WIDTH, sl] / jnp.maximum(ow_scr[KV_WIDTH:KV_WIDTH + 1, sl], 1e-30)
        o = (gs[3 * hh:3 * hh + 1, :] * oc_ref[0, 0, hh] + gs[3 * hh + 1:3 * hh + 2, :] * o_s
             + gs[3 * hh + 2:3 * hh + 3, :] * o_w)
        o = jnp.where((d_i >= g * NSA_HEAD_DIM) & (d_i < (g + 1) * NSA_HEAD_DIM), o, 0.0)
        ssq = ssq + jnp.sum(o * o, axis=0, keepdims=True)
        outs.append(o)
    rinv = lax.rsqrt(ssq / NSA_WIDTH + EPS)
    for hh in range(NSA_HEADS):
        on = outs[hh] * rinv * gain_ref[hh]
        o_ref[0, :, hh * LANES:(hh + 1) * LANES] = on.T.astype(BF16)


def _attn_gain(nsa_out_g):
    g = nsa_out_g.reshape(NSA_HEADS, NSA_HEAD_DIM)
    pad = []
    for hh in range(NSA_HEADS):
        z = jnp.zeros((NSA_HEAD_DIM,), F32)
        pad.append(jnp.concatenate([g[hh], z] if hh // NSA_GROUP == 0 else [z, g[hh]]))
    g = jnp.stack(pad)
    return jnp.broadcast_to(g[:, :, None], (NSA_HEADS, LANES, TQ))


def _attn_tables(T):
    tk = min(TK, T)
    wk = min(WKEYS, T)
    l = np.arange(TQ, dtype=np.float64)[None, :]
    rel = np.arange(tk, dtype=np.float64)[:, None] - l
    ar = np.stack([a * rel for a in ALIBI2]).astype(np.float32)
    dist = WINDOW + l - np.arange(wk, dtype=np.float64)[:, None]
    inside = (dist >= 0) & (dist < WINDOW)
    wb = np.stack([np.where(inside, -a * dist, NEG) for a in ALIBI2]).astype(np.float32)
    return jnp.asarray(ar), jnp.asarray(wb)


def _attn(q_hm, ks, vst, kw, vwt, sel, flags, oc, gt, gain_b):
    B, _, T, _ = q_hm.shape
    nsel = T // SLC_LEN
    nqt = T // TQ
    ar, wb = _attn_tables(T)
    tk, wk = ar.shape[1], wb.shape[1]
    once = dict(pipeline_mode=pl.Buffered(1))
    full_k = pl.BlockSpec((1, T, KV_WIDTH), lambda b, i, f: (b, 0, 0), **once)
    full_vt = pl.BlockSpec((1, T // LANES, VT_ROWS, LANES), lambda b, i, f: (b, 0, 0, 0), **once)
    const = lambda shape: pl.BlockSpec(shape, lambda b, i, f: (0,) * len(shape), **once)
    grid_spec = pltpu.PrefetchScalarGridSpec(
        num_scalar_prefetch=1,
        grid=(B, nqt),
        in_specs=[pl.BlockSpec((1, NSA_HEADS, TQ, LANES), lambda b, i, f: (b, 0, i, 0)),
                  full_k, full_vt, full_k, full_vt,
                  pl.BlockSpec((1, NSA_KV_HEADS, nsel, TQ), lambda b, i, f: (b, 0, 0, i)),
                  pl.BlockSpec((1, 1, NSA_HEADS, LANES, TQ), lambda b, i, f: (b, i, 0, 0, 0)),
                  pl.BlockSpec((1, GATE_ROWS, TQ), lambda b, i, f: (b, 0, i)),
                  const((NSA_HEADS, LANES, TQ)), const(ar.shape), const(wb.shape)],
        out_specs=pl.BlockSpec((1, TQ, QPAD), lambda b, i, f: (b, i, 0)),
        scratch_shapes=[pltpu.VMEM((VT_ROWS, NSA_HEADS * TQ), F32),
                        pltpu.VMEM((tk, NSA_HEADS * TQ), BF16),
                        pltpu.VMEM((1, NSA_HEADS * TQ), F32),
                        pltpu.VMEM((1, NSA_HEADS * TQ), F32),
                        pltpu.VMEM((wk, NSA_HEADS * TQ), BF16),
                        pltpu.VMEM((VT_ROWS, NSA_HEADS * TQ), F32)])
    return pl.pallas_call(
        _attn_kernel,
        grid_spec=grid_spec,
        out_shape=jax.ShapeDtypeStruct((B, T, QPAD), BF16),
        compiler_params=_cparams("parallel", "arbitrary"),
    )(flags.reshape(-1), q_hm, ks, vst, kw, vwt, sel, oc, gt, gain_b, ar, wb)


def _retention_kernel(q_ref, k_ref, v_ref, g_ref, dm_ref, qd_ref, kd_ref, gn_ref, o_ref, st_scr,
                      *, chunk_dec):
    rt = q_ref.shape[1]
    C, d = RET_CHUNK, RET_HEAD_DIM

    @pl.when(pl.program_id(1) == 0)
    def _():
        st_scr[...] = jnp.zeros_like(st_scr)

    for c in range(rt // C):
        rows = slice(c * C, (c + 1) * C)
        for h in range(RET_HEADS):
            cols = slice(h * d, (h + 1) * d)
            q = q_ref[0, rows, cols]
            k = k_ref[0, rows, cols]
            v = v_ref[0, rows, cols]
            state = st_scr[h]
            s = _mmg(q, k, _NT, preferred_element_type=F32) * dm_ref[h]
            inner = _mm(s.astype(BF16), v, preferred_element_type=F32)
            cross = _mm(q, state.astype(BF16), preferred_element_type=F32) * qd_ref[h]
            kd = (k.astype(F32) * kd_ref[h]).astype(BF16)
            st_scr[h] = state * chunk_dec[h] + _mmg(kd, v, _TN, preferred_element_type=F32)
            o = inner + cross
            mu = jnp.mean(o, axis=-1, keepdims=True)
            oc = o - mu
            var = jnp.mean(oc * oc, axis=-1, keepdims=True)
            o = oc * lax.rsqrt(var + EPS) * gn_ref[:, cols]
            gg = g_ref[0, rows, cols]
            o_ref[0, rows, cols] = (o * (gg * jax.nn.sigmoid(gg))).astype(BF16)


def _retention_consts():
    C, d = RET_CHUNK, RET_HEAD_DIM
    log_g = np.log(1.0 - np.exp2(-5.0 - np.arange(RET_HEADS, dtype=np.float64)))
    pos = np.arange(C, dtype=np.float64)
    diff = pos[:, None] - pos[None, :]
    scale = d ** -0.5
    dmask = np.where(diff >= 0, np.exp(np.maximum(diff, 0.0) * log_g[:, None, None]), 0.0) * scale
    q_dec = np.exp((pos + 1.0) * log_g[:, None])
    k_dec = np.exp((C - 1.0 - pos) * log_g[:, None]) * scale
    chunk_dec = tuple(float(x) for x in np.exp(C * log_g))
    bc = lambda a: jnp.asarray(np.broadcast_to(a[:, :, None], (RET_HEADS, C, d)).astype(np.float32))
    return jnp.asarray(dmask.astype(np.float32)), bc(q_dec), bc(k_dec), chunk_dec


def _retention(qr, kr, vr, gr, gn_gain):
    B, T, W = qr.shape
    rt = min(TM, T)
    dmask, qd, kd, chunk_dec = _retention_consts()
    row = pl.BlockSpec((1, rt, W), lambda b, i: (b, i, 0))
    cst = _const_spec((RET_HEADS, RET_CHUNK, RET_HEAD_DIM))
    return pl.pallas_call(
        functools.partial(_retention_kernel, chunk_dec=chunk_dec),
        grid=(B, T // rt),
        in_specs=[row, row, row, row, cst, cst, cst, _const_spec((1, W))],
        out_specs=row,
        out_shape=jax.ShapeDtypeStruct((B, T, W), BF16),
        scratch_shapes=[pltpu.VMEM((RET_HEADS, RET_HEAD_DIM, RET_HEAD_DIM), F32)],
        compiler_params=_cparams("parallel", "arbitrary"),
    )(qr, kr, vr, gr, dmask, qd, kd, gn_gain.reshape(1, W))


def _outproj_kernel(x_ref, on_ref, or_ref, wn_ref, wr_ref, post_ref, o_ref):
    y = (_mm(on_ref[0], wn_ref[...], preferred_element_type=F32)
         + _mm(or_ref[0], wr_ref[...], preferred_element_type=F32))
    o_ref[0] = x_ref[0] + _rms(y, post_ref[...])


def _outproj_weights(w_out):
    wn = w_out[:NSA_WIDTH].reshape(NSA_HEADS, NSA_HEAD_DIM, D_MODEL)
    rows = []
    for hh in range(NSA_HEADS):
        z = jnp.zeros((NSA_HEAD_DIM, D_MODEL), F32)
        rows += [wn[hh], z] if hh // NSA_GROUP == 0 else [z, wn[hh]]
    return jnp.concatenate(rows, axis=0).astype(BF16), w_out[NSA_WIDTH:].astype(BF16)


def _outproj(x, o_nsa, o_ret, wn, wr, post_g):
    B, T, D = x.shape
    tm = min(TM, T)
    row = lambda w: pl.BlockSpec((1, tm, w), lambda b, i: (b, i, 0))
    return pl.pallas_call(
        _outproj_kernel,
        grid=(B, T // tm),
        in_specs=[row(D), row(QPAD), row(RET_WIDTH), _const_spec(wn.shape), _const_spec(wr.shape),
                  _const_spec((1, D))],
        out_specs=row(D),
        out_shape=jax.ShapeDtypeStruct(x.shape, F32),
        compiler_params=_cparams("parallel", "parallel"),
    )(x, o_nsa, o_ret, wn, wr, post_g.reshape(1, D))


def _memkv_kernel(m_ref, g_ref, wk_ref, wv_ref, k_ref, v_ref):
    h = _rms(m_ref[0], g_ref[...]).astype(BF16)
    k_ref[0] = _mm(h, wk_ref[...], preferred_element_type=F32).astype(BF16)
    v_ref[0] = _mm(h, wv_ref[...], preferred_element_type=F32).astype(BF16)


def _memkv(mem, g, wk, wv):
    B, M, D = mem.shape
    blk = pl.BlockSpec((1, M, D), lambda b: (b, 0, 0))
    return pl.pallas_call(
        _memkv_kernel,
        grid=(B,),
        in_specs=[blk, _const_spec((1, D)), _const_spec(wk.shape), _const_spec(wv.shape)],
        out_specs=(blk, blk),
        out_shape=(jax.ShapeDtypeStruct(mem.shape, BF16), jax.ShapeDtypeStruct(mem.shape, BF16)),
        compiler_params=_cparams("parallel"),
    )(mem, g.reshape(1, D), wk, wv)


def _xattn_kernel(x_ref, pre_ref, wq_ref, k_ref, v_ref, wo_ref, post_ref, o_ref):
    x = x_ref[0]
    h = _rms(x, pre_ref[...]).astype(BF16)
    q = (_mm(h, wq_ref[...], preferred_element_type=F32) * (MEM_HEAD_DIM ** -0.5)).astype(BF16)
    heads = []
    for hd in range(MEM_HEADS):
        cols = slice(hd * MEM_HEAD_DIM, (hd + 1) * MEM_HEAD_DIM)
        s = _mmg(q[:, cols], k_ref[0, :, cols], _NT, preferred_element_type=F32)
        p = jnp.exp(s - jnp.max(s, axis=-1, keepdims=True))
        den = jnp.sum(p, axis=-1, keepdims=True)
        oh = _mm(p.astype(BF16), v_ref[0, :, cols], preferred_element_type=F32) / den
        heads.append(oh.astype(BF16))
    o = jnp.concatenate(heads, axis=1)
    y = _mm(o, wo_ref[...], preferred_element_type=F32)
    o_ref[0] = x + _rms(y, post_ref[...])


def _xattn(x, pre_g, wq, km, vm, wo, post_g):
    B, T, D = x.shape
    M = km.shape[1]
    tm = min(TM, T)
    row = pl.BlockSpec((1, tm, D), lambda b, i: (b, i, 0))
    kv = pl.BlockSpec((1, M, D), lambda b, i: (b, 0, 0))
    return pl.pallas_call(
        _xattn_kernel,
        grid=(B, T // tm),
        in_specs=[row, _const_spec((1, D)), _const_spec(wq.shape), kv, kv, _const_spec(wo.shape),
                  _const_spec((1, D))],
        out_specs=row,
        out_shape=jax.ShapeDtypeStruct(x.shape, F32),
        compiler_params=_cparams("parallel", "parallel"),
    )(x, pre_g.reshape(1, D), wq, km, vm, wo, post_g.reshape(1, D))


def _hybrid_mixer(x, pre_g, w_in, cmp_pe, cmp_w1, cmp_w2, nsa_out_g, ret_gn_g, w_out, post_g):
    q_hm, cmp, ks, vst, kw, vwt, gt, qr, kr, vr, gr = _inproj(x, pre_g, _inproj_weight(w_in))
    c_rm, c_t = _compress(cmp, *_compress_weights(cmp_pe, cmp_w1, cmp_w2))
    oc, sel, flags = _select(q_hm, c_rm, c_t)
    o_nsa = _attn(q_hm, ks, vst, kw, vwt, sel, flags, oc, gt, _attn_gain(nsa_out_g))
    o_ret = _retention(qr, kr, vr, gr, ret_gn_g)
    wn, wr = _outproj_weights(w_out)
    return _outproj(x, o_nsa, o_ret, wn, wr, post_g)


def kernel(x, mem, ffn1_pre_g, ffn1_w_gate, ffn1_w_up, ffn1_w_down, ffn1_post_g, mix_pre_g, w_in, cmp_pe, cmp_w1, cmp_w2, nsa_out_g, ret_gn_g, w_out, mix_post_g, xa_pre_g, xa_mem_g, xa_wq, xa_wk, xa_wv, xa_wo, xa_post_g, ffn2_pre_g, ffn2_w_gate, ffn2_w_up, ffn2_w_down, ffn2_post_g):
    depth = w_in.shape[0]
    bf = lambda w: w.astype(BF16)
    for l in range(depth):
        x = _ffn(x, ffn1_pre_g[l], bf(ffn1_w_gate[l]), bf(ffn1_w_up[l]), bf(ffn1_w_down[l]),
                 ffn1_post_g[l])
        x = _hybrid_mixer(x, mix_pre_g[l], w_in[l], cmp_pe[l], cmp_w1[l], cmp_w2[l], nsa_out_g[l],
                          ret_gn_g[l], w_out[l], mix_post_g[l])
        km, vm = _memkv(mem, xa_mem_g[l], bf(xa_wk[l]), bf(xa_wv[l]))
        x = _xattn(x, xa_pre_g[l], bf(xa_wq[l]), km, vm, bf(xa_wo[l]), xa_post_g[l])
        x = _ffn(x, ffn2_pre_g[l], bf(ffn2_w_gate[l]), bf(ffn2_w_up[l]), bf(ffn2_w_down[l]),
                 ffn2_post_g[l])
    return x
```

```python
import functools

import numpy as np
import jax
import jax.numpy as jnp
from jax import lax
from jax.experimental import pallas as pl
from jax.experimental.pallas import tpu as pltpu

F32 = jnp.float32
BF16 = jnp.bfloat16

D_MODEL = 1024
NSA_HEADS = 8
NSA_HEAD_DIM = 64
NSA_KV_HEADS = 2
NSA_GROUP = NSA_HEADS // NSA_KV_HEADS
NSA_WIDTH = NSA_HEADS * NSA_HEAD_DIM
KV_WIDTH = NSA_KV_HEADS * NSA_HEAD_DIM
CMP_LEN = 32
CMP_STRIDE = 16
CMP_HIDDEN = 256
SLC_LEN = 64
SLC_TOPN = 16
WINDOW = 512
RET_HEADS = 4
RET_HEAD_DIM = 128
RET_WIDTH = RET_HEADS * RET_HEAD_DIM
RET_CHUNK = 128
MEM_HEADS = 4
MEM_HEAD_DIM = D_MODEL // MEM_HEADS
D_FF = 2816
EPS = 1e-6
NEG = -1e30
TAKEN = -3e38
GATE_ROWS = 32

IN_SIZES = (NSA_WIDTH, KV_WIDTH, KV_WIDTH, KV_WIDTH, KV_WIDTH, KV_WIDTH, KV_WIDTH,
            NSA_HEADS * 3, RET_WIDTH, RET_WIDTH, RET_WIDTH, RET_WIDTH)
IN_OFFSETS = tuple(int(o) for o in np.cumsum(IN_SIZES)[:-1])

LANES = 128
QPAD = NSA_HEADS * LANES
LOG2E = 1.4426950408889634
ALIBI2 = tuple(float(2.0 ** (-8.0 * i / NSA_HEADS)) * LOG2E for i in range(1, NSA_HEADS + 1))
ONES_ROWS = 16
VT_ROWS = KV_WIDTH + ONES_ROWS
CH = 256

TM = 512
TQ = 128
TK = 512
WKEYS = WINDOW + TQ
VMEM_LIMIT = 56 * 1024 * 1024

_NT = (((1,), (1,)), ((), ()))
_TN = (((0,), (0,)), ((), ()))


def _mm(a, b, preferred_element_type=F32):
    return jnp.dot(a, b, preferred_element_type=preferred_element_type)


def _mmg(a, b, dims, preferred_element_type=F32):
    return lax.dot_general(a, b, dims, preferred_element_type=preferred_element_type)


def _cparams(*sem):
    return pltpu.CompilerParams(dimension_semantics=sem, vmem_limit_bytes=VMEM_LIMIT)


def _rms(x, g):
    return x * lax.rsqrt(jnp.mean(x * x, axis=-1, keepdims=True) + EPS) * g


def _const_spec(shape):
    nd = len(shape)
    return pl.BlockSpec(shape, lambda *_: (0,) * nd, pipeline_mode=pl.Buffered(1))


def _ffn_kernel(x_ref, pre_ref, wg_ref, wu_ref, wd_ref, post_ref, o_ref):
    x = x_ref[0]
    h = _rms(x, pre_ref[...]).astype(BF16)
    g = _mm(h, wg_ref[...], preferred_element_type=F32)
    u = _mm(h, wu_ref[...], preferred_element_type=F32)
    a = (g * jax.nn.sigmoid(g) * u).astype(BF16)
    y = _mm(a, wd_ref[...], preferred_element_type=F32)
    o_ref[0] = x + 0.5 * _rms(y, post_ref[...])


def _ffn(x, pre_g, wg, wu, wd, post_g):
    B, T, D = x.shape
    tm = min(TM, T)
    return pl.pallas_call(
        _ffn_kernel,
        grid=(B, T // tm),
        in_specs=[pl.BlockSpec((1, tm, D), lambda b, i: (b, i, 0)),
                  _const_spec((1, D)), _const_spec(wg.shape), _const_spec(wu.shape),
                  _const_spec(wd.shape), _const_spec((1, D))],
        out_specs=pl.BlockSpec((1, tm, D), lambda b, i: (b, i, 0)),
        out_shape=jax.ShapeDtypeStruct(x.shape, F32),
        compiler_params=_cparams("parallel", "parallel"),
    )(x, pre_g.reshape(1, D), wg, wu, wd, post_g.reshape(1, D))


_C_Q = 0
_C_CMP = _C_Q + QPAD
_C_KS = _C_CMP + 2 * KV_WIDTH
_C_VS = _C_KS + KV_WIDTH
_C_KW = _C_VS + KV_WIDTH
_C_VW = _C_KW + KV_WIDTH
_C_GATE = _C_VW + KV_WIDTH
_C_QR = _C_GATE + LANES
_C_KR = _C_QR + RET_WIDTH
_C_VR = _C_KR + RET_WIDTH
_C_GR = _C_VR + RET_WIDTH
_C_END = _C_GR + RET_WIDTH


def _inproj_kernel(x_ref, pre_ref, w_ref, q_ref, cmp_ref, ks_ref, vst_ref, kw_ref, vwt_ref,
                   gt_ref, qr_ref, kr_ref, vr_ref, gr_ref):
    tm = x_ref.shape[1]
    h = _rms(x_ref[0], pre_ref[...]).astype(BF16)

    def proj(lo, hi):
        return _mm(h, w_ref[:, lo:hi], preferred_element_type=F32)

    q = proj(_C_Q, _C_CMP)
    for j in range(tm // TQ):
        for hh in range(NSA_HEADS):
            q_ref[0, j, :, hh * TQ:(hh + 1) * TQ] = (
                q[j * TQ:(j + 1) * TQ, hh * LANES:(hh + 1) * LANES].T.astype(BF16))
    c = proj(_C_CMP, _C_KS)
    cmp_ref[0, 0] = c[:, :KV_WIDTH].astype(BF16)
    cmp_ref[0, 1] = c[:, KV_WIDTH:].astype(BF16)
    ks_ref[0] = proj(_C_KS, _C_VS).astype(BF16)
    kw_ref[0] = proj(_C_KW, _C_VW).astype(BF16)
    vs = proj(_C_VS, _C_KW)
    vw = proj(_C_VW, _C_GATE)
    ones = jnp.ones((ONES_ROWS, LANES), BF16)
    for j in range(tm // LANES):
        vst_ref[0, j, :KV_WIDTH, :] = vs[j * LANES:(j + 1) * LANES, :].T.astype(BF16)
        vwt_ref[0, j, :KV_WIDTH, :] = vw[j * LANES:(j + 1) * LANES, :].T.astype(BF16)
        vst_ref[0, j, KV_WIDTH:, :] = ones
        vwt_ref[0, j, KV_WIDTH:, :] = ones
    gt_ref[0] = proj(_C_GATE, _C_QR).T[:GATE_ROWS, :]
    qr_ref[0] = proj(_C_QR, _C_KR).astype(BF16)
    kr_ref[0] = proj(_C_KR, _C_VR).astype(BF16)
    vr_ref[0] = proj(_C_VR, _C_GR).astype(BF16)
    gr_ref[0] = proj(_C_GR, _C_END)


def _inproj_weight(w_in):
    parts = jnp.split(w_in, IN_OFFSETS, axis=-1)
    q_n, kc, vc, ks, vs, kw, vw, gates, q_r, k_r, v_r, g_r = parts
    scale = NSA_HEAD_DIM ** -0.5 * LOG2E
    qcols = []
    for hh in range(NSA_HEADS):
        g = hh // NSA_GROUP
        blk = q_n[:, hh * NSA_HEAD_DIM:(hh + 1) * NSA_HEAD_DIM] * scale
        z = jnp.zeros_like(blk)
        qcols += [blk, z] if g == 0 else [z, blk]
    gates_p = jnp.pad(gates, ((0, 0), (0, LANES - gates.shape[1])))
    w = jnp.concatenate(qcols + [kc, vc, ks, vs, kw, vw, gates_p, q_r, k_r, v_r, g_r], axis=1)
    assert w.shape[1] == _C_END
    return w.astype(BF16)


def _inproj(x, pre_g, w_all):
    B, T, D = x.shape
    tm = min(TM, T)
    nb = tm // LANES
    row = lambda w: pl.BlockSpec((1, tm, w), lambda b, i: (b, i, 0))
    out_shape = (
        jax.ShapeDtypeStruct((B, T // TQ, LANES, NSA_HEADS * TQ), BF16),
        jax.ShapeDtypeStruct((B, 2, T, KV_WIDTH), BF16),
        jax.ShapeDtypeStruct((B, T, KV_WIDTH), BF16),
        jax.ShapeDtypeStruct((B, T // LANES, VT_ROWS, LANES), BF16),
        jax.ShapeDtypeStruct((B, T, KV_WIDTH), BF16),
        jax.ShapeDtypeStruct((B, T // LANES, VT_ROWS, LANES), BF16),
        jax.ShapeDtypeStruct((B, GATE_ROWS, T), F32),
        jax.ShapeDtypeStruct((B, T, RET_WIDTH), BF16),
        jax.ShapeDtypeStruct((B, T, RET_WIDTH), BF16),
        jax.ShapeDtypeStruct((B, T, RET_WIDTH), BF16),
        jax.ShapeDtypeStruct((B, T, RET_WIDTH), F32),
    )
    vt_spec = pl.BlockSpec((1, nb, VT_ROWS, LANES), lambda b, i: (b, i, 0, 0))
    out_specs = (
        pl.BlockSpec((1, tm // TQ, LANES, NSA_HEADS * TQ), lambda b, i: (b, i, 0, 0)),
        pl.BlockSpec((1, 2, tm, KV_WIDTH), lambda b, i: (b, 0, i, 0)),
        row(KV_WIDTH), vt_spec, row(KV_WIDTH), vt_spec,
        pl.BlockSpec((1, GATE_ROWS, tm), lambda b, i: (b, 0, i)),
        row(RET_WIDTH), row(RET_WIDTH), row(RET_WIDTH), row(RET_WIDTH),
    )
    return pl.pallas_call(
        _inproj_kernel,
        grid=(B, T // tm),
        in_specs=[pl.BlockSpec((1, tm, D), lambda b, i: (b, i, 0)),
                  _const_spec((1, D)), _const_spec(w_all.shape)],
        out_specs=out_specs,
        out_shape=out_shape,
        compiler_params=_cparams("parallel", "parallel"),
    )(x, pre_g.reshape(1, D), w_all)


def _compress_kernel(a_ref, pe_ref, w1_ref, w2_ref, rm_ref, t_ref, hi_scr):
    ncp = a_ref.shape[2]
    a = a_ref[0, 0].astype(F32)
    a_lo = (a + pe_ref[0, 0:1, :]).astype(BF16)
    a_hi = (a + pe_ref[0, 1:2, :]).astype(BF16)
    c = jnp.zeros((ncp, KV_WIDTH), F32)
    hi_scr[pl.ds(ncp, 8), :] = jnp.zeros((8, CMP_HIDDEN), F32)
    for g in range(NSA_KV_HEADS):
        p_lo = _mm(a_lo, w1_ref[0, 2 * g], preferred_element_type=F32)
        hi_scr[pl.ds(0, ncp), :] = _mm(a_hi, w1_ref[0, 2 * g + 1], preferred_element_type=F32)
        hdn = p_lo + hi_scr[pl.ds(1, ncp), :]
        hdn = (hdn * jax.nn.sigmoid(hdn)).astype(BF16)
        c = c + _mm(hdn, w2_ref[0, g], preferred_element_type=F32)
    rows = lax.broadcasted_iota(jnp.int32, c.shape, 0)
    c = jnp.where(rows < ncp - 1, c, 0.0)
    rm_ref[0, 0] = c.astype(BF16)
    for j in range(ncp // CH):
        t_ref[0, 0, j] = c[j * CH:(j + 1) * CH, :].T.astype(BF16)


def _compress_weights(cmp_pe, cmp_w1, cmp_w2):
    half = CMP_LEN // 2
    pe = cmp_pe.reshape(2, 2, half, 1, NSA_HEAD_DIM)
    pe = jnp.broadcast_to(pe, (2, 2, half, NSA_KV_HEADS, NSA_HEAD_DIM)).reshape(2, 2, half * KV_WIDTH)
    w1 = cmp_w1.reshape(2, 2, half, NSA_HEAD_DIM, CMP_HIDDEN)
    w1e = []
    for g in range(NSA_KV_HEADS):
        for hf in range(2):
            z = jnp.zeros((2, half, NSA_KV_HEADS, NSA_HEAD_DIM, CMP_HIDDEN), F32)
            z = z.at[:, :, g].set(w1[:, hf])
            w1e.append(z.reshape(2, half * KV_WIDTH, CMP_HIDDEN))
    w1e = jnp.stack(w1e, axis=1).astype(BF16)
    w2e = []
    for g in range(NSA_KV_HEADS):
        z = jnp.zeros((2, CMP_HIDDEN, NSA_KV_HEADS, NSA_HEAD_DIM), F32)
        z = z.at[:, :, g].set(cmp_w2)
        w2e.append(z.reshape(2, CMP_HIDDEN, KV_WIDTH))
    w2e = jnp.stack(w2e, axis=1).astype(BF16)
    return pe, w1e, w2e


def _compress(cmp, pe, w1e, w2e):
    B, _, T, _ = cmp.shape
    ncp = T // CMP_STRIDE
    a = cmp.reshape(B, 2, ncp, CMP_STRIDE * KV_WIDTH)
    kw = CMP_STRIDE * KV_WIDTH
    return pl.pallas_call(
        _compress_kernel,
        grid=(B, 2),
        in_specs=[pl.BlockSpec((1, 1, ncp, kw), lambda b, k: (b, k, 0, 0)),
                  pl.BlockSpec((1, 2, kw), lambda b, k: (k, 0, 0)),
                  pl.BlockSpec((1, 4, kw, CMP_HIDDEN), lambda b, k: (k, 0, 0, 0)),
                  pl.BlockSpec((1, 2, CMP_HIDDEN, KV_WIDTH), lambda b, k: (k, 0, 0, 0))],
        out_specs=(pl.BlockSpec((1, 1, ncp, KV_WIDTH), lambda b, k: (b, k, 0, 0)),
                   pl.BlockSpec((1, 1, ncp // CH, KV_WIDTH, CH), lambda b, k: (b, k, 0, 0, 0))),
        out_shape=(jax.ShapeDtypeStruct((B, 2, ncp, KV_WIDTH), BF16),
                   jax.ShapeDtypeStruct((B, 2, ncp // CH, KV_WIDTH, CH), BF16)),
        scratch_shapes=[pltpu.VMEM((ncp + 8, CMP_HIDDEN), F32)],
        compiler_params=_cparams("parallel", "parallel"),
    )(a, pe, w1e, w2e)


def _select_kernel(q_ref, kc_ref, vct_ref, ovl_ref, tab_ref, grp_ref, oc_ref, sel_ref, flag_ref,
                   s_scr, p_scr, imp_scr, oc_scr):
    tq = TQ
    nsel = ovl_ref.shape[1]
    q0 = pl.program_id(1) * tq
    qt = q_ref[0, 0]
    span = CH * CMP_STRIDE
    nch = (q0 + tq - CMP_LEN) // span + 1
    nfull = jnp.maximum((q0 - (span + CMP_LEN - CMP_STRIDE - 1)) // span + 1, 0)
    mrel = (lax.broadcasted_iota(jnp.int32, (CH, tq), 0) * CMP_STRIDE + (CMP_LEN - 1)
            - lax.broadcasted_iota(jnp.int32, (CH, tq), 1))
    heads = [slice(hh * tq, (hh + 1) * tq) for hh in range(NSA_HEADS)]

    def chunk_rows(c):
        return pl.ds(pl.multiple_of(c * CH, CH), CH)

    def scores(masked):
        def body(c, ms):
            rows = chunk_rows(c)
            s_c = _mm(kc_ref[0, 0, rows, :], qt, preferred_element_type=F32)
            ok = mrel <= q0 - c * span
            out = []
            for hh, sl in enumerate(heads):
                s = s_c[:, sl] + tab_ref[hh, rows, :]
                if masked:
                    s = jnp.where(ok, s, NEG)
                s_scr[rows, sl] = s
                out.append(jnp.maximum(ms[hh], jnp.max(s, axis=0, keepdims=True)))
            return tuple(out)
        return body

    m0 = tuple(jnp.full((1, tq), NEG, F32) for _ in heads)
    ms = lax.fori_loop(0, nfull, scores(False), m0)
    ms = lax.fori_loop(nfull, nch, scores(True), ms)

    def probs(masked):
        def body(c, dens):
            rows = chunk_rows(c)
            ok = mrel <= q0 - c * span
            out = []
            for hh, sl in enumerate(heads):
                p = jnp.exp2(s_scr[rows, sl] - ms[hh])
                if masked:
                    p = jnp.where(ok, p, 0.0)
                s_scr[rows, sl] = p
                out.append(dens[hh] + jnp.sum(p, axis=0, keepdims=True))
            return tuple(out)
        return body

    d0 = tuple(jnp.zeros((1, tq), F32) for _ in heads)
    dens = lax.fori_loop(0, nfull, probs(False), d0)
    dens = lax.fori_loop(nfull, nch, probs(True), dens)
    rden = [1.0 / jnp.maximum(d, 1e-30) for d in dens]

    imp_scr[...] = jnp.zeros_like(imp_scr)
    oc_scr[...] = jnp.zeros_like(oc_scr)

    def outputs(c, carry):
        rows = chunk_rows(c)
        for g in range(NSA_KV_HEADS):
            psum = jnp.zeros((CH, tq), F32)
            for r in range(NSA_GROUP):
                hh = g * NSA_GROUP + r
                pn = s_scr[rows, heads[hh]] * rden[hh]
                psum = psum + pn
                p_scr[:, heads[hh]] = pn.astype(BF16)
            p_hi = psum.astype(BF16)
            p_lo = (psum - p_hi.astype(F32)).astype(BF16)
            imp_scr[g] += (_mm(ovl_ref[c], p_hi, preferred_element_type=F32)
                           + _mm(ovl_ref[c], p_lo, preferred_element_type=F32))
        oc_scr[...] += _mm(vct_ref[0, 0, c], p_scr[...], preferred_element_type=F32)
        return carry

    lax.fori_loop(0, nch, outputs, 0)
    for hh, sl in enumerate(heads):
        oc_ref[0, 0, hh] = oc_scr[:, sl]

    n_top = min(SLC_TOPN, nsel)
    n_seen = (q0 + tq - 1) // SLC_LEN + 1
    sizes = sorted({r for r in (nsel // 8, nsel // 4, nsel // 2, nsel) if r >= n_top and r % 8 == 0})

    def select(rows):
        jb = lax.broadcasted_iota(jnp.int32, (rows, tq), 0)
        cur = (q0 + lax.broadcasted_iota(jnp.int32, (rows, tq), 1)) // SLC_LEN
        forced = (jb == 0) | (jb == cur) | (jb == cur - 1)
        cur1 = (q0 + lax.broadcasted_iota(jnp.int32, (1, tq), 1)) // SLC_LEN
        quota = n_top - (1 + (cur1 >= 1).astype(jnp.int32) + (cur1 >= 2).astype(jnp.int32))
        imps = tuple(jnp.where(forced, TAKEN, jnp.where(jb <= cur, imp_scr[g, pl.ds(0, rows), :], -1.0))
                     for g in range(NSA_KV_HEADS))

        def pick(i, imps):
            out = []
            for imp in imps:
                best = jnp.max(imp, axis=0, keepdims=True)
                first = jnp.min(jnp.where(imp == best, jb, rows), axis=0, keepdims=True)
                first = jnp.where(i < quota, first, -1)
                out.append(jnp.where(jb == first, TAKEN, imp))
            return tuple(out)

        rounds = n_top - jnp.where(q0 >= 2 * SLC_LEN, 3, 1)
        imps = lax.fori_loop(0, rounds, pick, imps)
        ones = jnp.ones((ONES_ROWS, tq), BF16)
        cnt = jnp.zeros((ONES_ROWS, rows), F32)
        for g in range(NSA_KV_HEADS):
            taken = imps[g] < -2.0
            sel_ref[0, g, pl.ds(0, rows), :] = jnp.where(taken, 0.0, NEG)
            if rows < nsel:
                sel_ref[0, g, pl.ds(rows, nsel - rows), :] = jnp.full((nsel - rows, tq), NEG, F32)
            cnt = cnt + _mmg(ones, jnp.where(taken, 1.0, 0.0).astype(BF16), _NT,
                             preferred_element_type=F32)
        act = _mm(jnp.minimum(cnt, 1.0).astype(BF16), grp_ref[pl.ds(0, rows), :],
                  preferred_element_type=F32)
        flag_ref[0, 0] = (act[0:1, :] > 0.0).astype(jnp.int32)

    lo = 0
    for rows in sizes:
        cond = n_seen > lo
        if rows < nsel:
            cond = cond & (n_seen <= rows)
        pl.when(cond)(functools.partial(select, rows))
        lo = rows


def _overlap_t(T):
    ncp = T // CMP_STRIDE
    nc = (T - CMP_LEN) // CMP_STRIDE + 1
    nsel = T // SLC_LEN
    c_start = np.arange(nc) * CMP_STRIDE
    c_end = c_start + CMP_LEN - 1
    s_start = np.arange(nsel) * SLC_LEN
    s_end = s_start + SLC_LEN - 1
    ov = np.clip(np.minimum(c_end[:, None], s_end[None, :])
                 - np.maximum(c_start[:, None], s_start[None, :]) + 1, 0, None) / CMP_LEN
    out = np.zeros((nsel, ncp), np.float32)
    out[:, :nc] = ov.T
    out = out.reshape(nsel, ncp // CH, CH).transpose(1, 0, 2)
    return jnp.asarray(out, dtype=BF16)


def _select_tables(T):
    ncp = T // CMP_STRIDE
    nsel = T // SLC_LEN
    nkt = T // min(TK, T)
    n = np.arange(ncp, dtype=np.float64)[:, None]
    l = np.arange(TQ, dtype=np.float64)[None, :]
    dist = l - (n * CMP_STRIDE + (CMP_LEN - 1) / 2.0)
    tab = np.stack([-a * dist for a in ALIBI2]).astype(np.float32)
    grp = (np.arange(nsel)[:, None] // (nsel // nkt) == np.arange(nkt)[None, :]).astype(np.float32)
    grp[0, :] = 0.0
    return jnp.asarray(tab), jnp.asarray(grp, dtype=BF16)


def _select(q_hm, c_rm, c_t):
    B, T = q_hm.shape[0], q_hm.shape[1] * TQ
    ncp = T // CMP_STRIDE
    nsel = T // SLC_LEN
    nqt = T // TQ
    nkt = T // min(TK, T)
    tab, grp = _select_tables(T)
    return pl.pallas_call(
        _select_kernel,
        grid=(B, nqt),
        in_specs=[pl.BlockSpec((1, 1, LANES, NSA_HEADS * TQ), lambda b, i: (b, i, 0, 0)),
                  pl.BlockSpec((1, 1, ncp, KV_WIDTH), lambda b, i: (b, 0, 0, 0)),
                  pl.BlockSpec((1, 1, ncp // CH, KV_WIDTH, CH), lambda b, i: (b, 1, 0, 0, 0)),
                  _const_spec((ncp // CH, nsel, CH)), _const_spec(tab.shape), _const_spec(grp.shape)],
        out_specs=(pl.BlockSpec((1, 1, NSA_HEADS, LANES, TQ), lambda b, i: (b, i, 0, 0, 0)),
                   pl.BlockSpec((1, NSA_KV_HEADS, nsel, TQ), lambda b, i: (b, 0, 0, i)),
                   pl.BlockSpec((1, 1, 1, nkt), lambda b, i: (b, i, 0, 0))),
        out_shape=(jax.ShapeDtypeStruct((B, nqt, NSA_HEADS, LANES, TQ), F32),
                   jax.ShapeDtypeStruct((B, NSA_KV_HEADS, nsel, T), F32),
                   jax.ShapeDtypeStruct((B, nqt, 1, nkt), jnp.int32)),
        scratch_shapes=[pltpu.VMEM((ncp, NSA_HEADS * TQ), F32),
                        pltpu.VMEM((CH, NSA_HEADS * TQ), BF16),
                        pltpu.VMEM((NSA_KV_HEADS, nsel, TQ), F32),
                        pltpu.VMEM((KV_WIDTH, NSA_HEADS * TQ), F32)],
        compiler_params=_cparams("parallel", "arbitrary"),
    )(q_hm, c_rm, c_t, _overlap_t(T), tab, grp)


def _attn_kernel(flags_ref, q_ref, ks_ref, vst_ref, kw_ref, vwt_ref, sel_ref, oc_ref, gt_ref,
                 gain_ref, ar_ref, wb_ref, o_ref, acc_scr, p_scr, m_scr, alpha_scr, pw_scr, ow_scr):
    tq = TQ
    T = ks_ref.shape[1]
    qi = pl.program_id(1)
    q0 = qi * tq
    tk = ar_ref.shape[1]
    wk = wb_ref.shape[1]
    nkt = T // tk
    bpt = tk // SLC_LEN
    vpt = tk // LANES
    qt = q_ref[0, 0]
    heads = [slice(hh * tq, (hh + 1) * tq) for hh in range(NSA_HEADS)]

    def block_bias(sb, nblk, hh, shift):
        sbh = sb - ALIBI2[hh] * shift
        return jnp.concatenate(
            [jnp.broadcast_to(sbh[j:j + 1, :], (SLC_LEN, tq)) for j in range(nblk)], axis=0)

    def tile(kt, rows, first):
        k0 = kt * tk
        s_all = _mm(ks_ref[0, pl.ds(pl.multiple_of(k0, tk), rows), :], qt,
                    preferred_element_type=F32)
        vt = jnp.concatenate([vst_ref[0, kt * vpt + j] for j in range(rows // LANES)], axis=1)
        shift = (q0 - k0).astype(F32)
        if first:
            causal = (lax.broadcasted_iota(jnp.int32, (rows, tq), 0) + (k0 - q0)
                      <= lax.broadcasted_iota(jnp.int32, (rows, tq), 1))
        for g in range(NSA_KV_HEADS):
            sb = sel_ref[0, g, pl.ds(pl.multiple_of(kt * bpt, bpt), bpt), :]
            for r in range(NSA_GROUP):
                hh = g * NSA_GROUP + r
                sl = heads[hh]
                s = (s_all[:, sl] + ar_ref[hh, pl.ds(0, rows), :]
                     + block_bias(sb, rows // SLC_LEN, hh, shift))
                if first:
                    s = jnp.where(causal, s, NEG)
                    m_new = jnp.max(s, axis=0, keepdims=True)
                else:
                    m_old = m_scr[:, sl]
                    m_new = jnp.maximum(m_old, jnp.max(s, axis=0, keepdims=True))
                    alpha_scr[:, sl] = jnp.exp2(m_old - m_new)
                m_scr[:, sl] = m_new
                p_scr[pl.ds(0, rows), sl] = jnp.exp2(s - m_new).astype(BF16)
        pv = _mm(vt, p_scr[pl.ds(0, rows), :], preferred_element_type=F32)
        if first:
            acc_scr[...] = pv
        else:
            acc_scr[...] = acc_scr[...] * alpha_scr[...] + pv

    kt_diag = q0 // tk
    fbase = (pl.program_id(0) * pl.num_programs(1) + qi) * nkt
    for jj in range(tk // tq):
        pl.when(qi % (tk // tq) == jj)(functools.partial(tile, kt_diag, (jj + 1) * tq, True))

    def body(kt, carry):
        @pl.when(flags_ref[fbase + kt] != 0)
        def _():
            tile(kt, tk, False)
        return carry

    lax.fori_loop(0, kt_diag, body, 0)

    pl.when((kt_diag > 0) & (flags_ref[fbase] == 0))(
        functools.partial(tile, qi * 0, LANES, False))

    def window(ws, general):
        sw_all = _mm(kw_ref[0, pl.ds(pl.multiple_of(ws, LANES), wk), :], qt,
                     preferred_element_type=F32)
        vwt = jnp.concatenate([vwt_ref[0, ws // LANES + j] for j in range(wk // LANES)], axis=1)
        if general:
            kpos = ws + lax.broadcasted_iota(jnp.int32, (wk, tq), 0)
            dist = q0 + lax.broadcasted_iota(jnp.int32, (wk, tq), 1) - kpos
            wmask = (dist >= 0) & (dist < WINDOW)
            distf = dist.astype(F32)
        for hh, sl in enumerate(heads):
            if general:
                s = jnp.where(wmask, sw_all[:, sl] - ALIBI2[hh] * distf, NEG)
            else:
                s = sw_all[:, sl] + wb_ref[hh]
            pw_scr[:, sl] = jnp.exp2(s - jnp.max(s, axis=0, keepdims=True)).astype(BF16)
        ow_scr[...] = _mm(vwt, pw_scr[...], preferred_element_type=F32)

    @pl.when(q0 >= WINDOW)
    def _():
        window(q0 - WINDOW, False)

    @pl.when(q0 < WINDOW)
    def _():
        window(q0 * 0, True)

    gs = jax.nn.sigmoid(gt_ref[0])
    d_i = lax.broadcasted_iota(jnp.int32, (LANES, tq), 0)
    outs = []
    ssq = jnp.zeros((1, tq), F32)
    for hh in range(NSA_HEADS):
        g = hh // NSA_GROUP
        sl = heads[hh]
        o_s = acc_scr[:KV_WIDTH, sl] / jnp.maximum(acc_scr[KV_WIDTH:KV_WIDTH + 1, sl], 1e-30)
        o_w = ow_scr[:KV_WIDTH, sl] / jnp.maximum(ow_scr[KV_WIDTH:KV_WIDTH + 1, sl], 1e-30)
        o = (gs[3 * hh:3 * hh + 1, :] * oc_ref[0, 0, hh] + gs[3 * hh + 1:3 * hh + 2, :] * o_s
             + gs[3 * hh + 2:3 * hh + 3, :] * o_w)
        o = jnp.where((d_i >= g * NSA_HEAD_DIM) & (d_i < (g + 1) * NSA_HEAD_DIM), o, 0.0)
        ssq = ssq + jnp.sum(o * o, axis=0, keepdims=True)
        outs.append(o)
    rinv = lax.rsqrt(ssq / NSA_WIDTH + EPS)
    for hh in range(NSA_HEADS):
        on = outs[hh] * rinv * gain_ref[hh]
        o_ref[0, :, hh * LANES:(hh + 1) * LANES] = on.T.astype(BF16)


def _attn_gain(nsa_out_g):
    g = nsa_out_g.reshape(NSA_HEADS, NSA_HEAD_DIM)
    pad = []
    for hh in range(NSA_HEADS):
        z = jnp.zeros((NSA_HEAD_DIM,), F32)
        pad.append(jnp.concatenate([g[hh], z] if hh // NSA_GROUP == 0 else [z, g[hh]]))
    g = jnp.stack(pad)
    return jnp.broadcast_to(g[:, :, None], (NSA_HEADS, LANES, TQ))


def _attn_tables(T):
    tk = min(TK, T)
    wk = min(WKEYS, T)
    l = np.arange(TQ, dtype=np.float64)[None, :]
    rel = np.arange(tk, dtype=np.float64)[:, None] - l
    ar = np.stack([a * rel for a in ALIBI2]).astype(np.float32)
    dist = WINDOW + l - np.arange(wk, dtype=np.float64)[:, None]
    inside = (dist >= 0) & (dist < WINDOW)
    wb = np.stack([np.where(inside, -a * dist, NEG) for a in ALIBI2]).astype(np.float32)
    return jnp.asarray(ar), jnp.asarray(wb)


def _attn(q_hm, ks, vst, kw, vwt, sel, flags, oc, gt, gain_b):
    B, T = q_hm.shape[0], q_hm.shape[1] * TQ
    nsel = T // SLC_LEN
    nqt = T // TQ
    ar, wb = _attn_tables(T)
    tk, wk = ar.shape[1], wb.shape[1]
    once = dict(pipeline_mode=pl.Buffered(1))
    full_k = pl.BlockSpec((1, T, KV_WIDTH), lambda b, i, f: (b, 0, 0), **once)
    full_vt = pl.BlockSpec((1, T // LANES, VT_ROWS, LANES), lambda b, i, f: (b, 0, 0, 0), **once)
    const = lambda shape: pl.BlockSpec(shape, lambda b, i, f: (0,) * len(shape), **once)
    grid_spec = pltpu.PrefetchScalarGridSpec(
        num_scalar_prefetch=1,
        grid=(B, nqt),
        in_specs=[pl.BlockSpec((1, 1, LANES, NSA_HEADS * TQ), lambda b, i, f: (b, i, 0, 0)),
                  full_k, full_vt, full_k, full_vt,
                  pl.BlockSpec((1, NSA_KV_HEADS, nsel, TQ), lambda b, i, f: (b, 0, 0, i)),
                  pl.BlockSpec((1, 1, NSA_HEADS, LANES, TQ), lambda b, i, f: (b, i, 0, 0, 0)),
                  pl.BlockSpec((1, GATE_ROWS, TQ), lambda b, i, f: (b, 0, i)),
                  const((NSA_HEADS, LANES, TQ)), const(ar.shape), const(wb.shape)],
        out_specs=pl.BlockSpec((1, TQ, QPAD), lambda b, i, f: (b, i, 0)),
        scratch_shapes=[pltpu.VMEM((VT_ROWS, NSA_HEADS * TQ), F32),
                        pltpu.VMEM((tk, NSA_HEADS * TQ), BF16),
                        pltpu.VMEM((1, NSA_HEADS * TQ), F32),
                        pltpu.VMEM((1, NSA_HEADS * TQ), F32),
                        pltpu.VMEM((wk, NSA_HEADS * TQ), BF16),
                        pltpu.VMEM((VT_ROWS, NSA_HEADS * TQ), F32)])
    return pl.pallas_call(
        _attn_kernel,
        grid_spec=grid_spec,
        out_shape=jax.ShapeDtypeStruct((B, T, QPAD), BF16),
        compiler_params=_cparams("parallel", "arbitrary"),
    )(flags.reshape(-1), q_hm, ks, vst, kw, vwt, sel, oc, gt, gain_b, ar, wb)


def _retention_kernel(q_ref, k_ref, v_ref, g_ref, dm_ref, qd_ref, kd_ref, gn_ref, o_ref, st_scr,
                      *, chunk_dec):
    rt = q_ref.shape[1]
    C, d = RET_CHUNK, RET_HEAD_DIM

    @pl.when(pl.program_id(1) == 0)
    def _():
        st_scr[...] = jnp.zeros_like(st_scr)

    for c in range(rt // C):
        rows = slice(c * C, (c + 1) * C)
        for h in range(RET_HEADS):
            cols = slice(h * d, (h + 1) * d)
            q = q_ref[0, rows, cols]
            k = k_ref[0, rows, cols]
            v = v_ref[0, rows, cols]
            state = st_scr[h]
            s = _mmg(q, k, _NT, preferred_element_type=F32) * dm_ref[h]
            inner = _mm(s.astype(BF16), v, preferred_element_type=F32)
            cross = _mm(q, state.astype(BF16), preferred_element_type=F32) * qd_ref[h]
            kd = (k.astype(F32) * kd_ref[h]).astype(BF16)
            st_scr[h] = state * chunk_dec[h] + _mmg(kd, v, _TN, preferred_element_type=F32)
            o = inner + cross
            mu = jnp.mean(o, axis=-1, keepdims=True)
            oc = o - mu
            var = jnp.mean(oc * oc, axis=-1, keepdims=True)
            o = oc * lax.rsqrt(var + EPS) * gn_ref[:, cols]
            gg = g_ref[0, rows, cols]
            o_ref[0, rows, cols] = (o * (gg * jax.nn.sigmoid(gg))).astype(BF16)


def _retention_consts():
    C, d = RET_CHUNK, RET_HEAD_DIM
    log_g = np.log(1.0 - np.exp2(-5.0 - np.arange(RET_HEADS, dtype=np.float64)))
    pos = np.arange(C, dtype=np.float64)
    diff = pos[:, None] - pos[None, :]
    scale = d ** -0.5
    dmask = np.where(diff >= 0, np.exp(np.maximum(diff, 0.0) * log_g[:, None, None]), 0.0) * scale
    q_dec = np.exp((pos + 1.0) * log_g[:, None])
    k_dec = np.exp((C - 1.0 - pos) * log_g[:, None]) * scale
    chunk_dec = tuple(float(x) for x in np.exp(C * log_g))
    bc = lambda a: jnp.asarray(np.broadcast_to(a[:, :, None], (RET_HEADS, C, d)).astype(np.float32))
    return jnp.asarray(dmask.astype(np.float32)), bc(q_dec), bc(k_dec), chunk_dec


def _retention(qr, kr, vr, gr, gn_gain):
    B, T, W = qr.shape
    rt = min(TM, T)
    dmask, qd, kd, chunk_dec = _retention_consts()
    row = pl.BlockSpec((1, rt, W), lambda b, i: (b, i, 0))
    cst = _const_spec((RET_HEADS, RET_CHUNK, RET_HEAD_DIM))
    return pl.pallas_call(
        functools.partial(_retention_kernel, chunk_dec=chunk_dec),
        grid=(B, T // rt),
        in_specs=[row, row, row, row, cst, cst, cst, _const_spec((1, W))],
        out_specs=row,
        out_shape=jax.ShapeDtypeStruct((B, T, W), BF16),
        scratch_shapes=[pltpu.VMEM((RET_HEADS, RET_HEAD_DIM, RET_HEAD_DIM), F32)],
        compiler_params=_cparams("parallel", "arbitrary"),
    )(qr, kr, vr, gr, dmask, qd, kd, gn_gain.reshape(1, W))


def _outproj_kernel(x_ref, on_ref, or_ref, wn_ref, wr_ref, post_ref, o_ref):
    y = (_mm(on_ref[0], wn_ref[...], preferred_element_type=F32)
         + _mm(or_ref[0], wr_ref[...], preferred_element_type=F32))
    o_ref[0] = x_ref[0] + _rms(y, post_ref[...])


def _outproj_weights(w_out):
    wn = w_out[:NSA_WIDTH].reshape(NSA_HEADS, NSA_HEAD_DIM, D_MODEL)
    rows = []
    for hh in range(NSA_HEADS):
        z = jnp.zeros((NSA_HEAD_DIM, D_MODEL), F32)
        rows += [wn[hh], z] if hh // NSA_GROUP == 0 else [z, wn[hh]]
    return jnp.concatenate(rows, axis=0).astype(BF16), w_out[NSA_WIDTH:].astype(BF16)


def _outproj(x, o_nsa, o_ret, wn, wr, post_g):
    B, T, D = x.shape
    tm = min(TM, T)
    row = lambda w: pl.BlockSpec((1, tm, w), lambda b, i: (b, i, 0))
    return pl.pallas_call(
        _outproj_kernel,
        grid=(B, T // tm),
        in_specs=[row(D), row(QPAD), row(RET_WIDTH), _const_spec(wn.shape), _const_spec(wr.shape),
                  _const_spec((1, D))],
        out_specs=row(D),
        out_shape=jax.ShapeDtypeStruct(x.shape, F32),
        compiler_params=_cparams("parallel", "parallel"),
    )(x, o_nsa, o_ret, wn, wr, post_g.reshape(1, D))


def _memkv_kernel(m_ref, g_ref, wk_ref, wv_ref, k_ref, v_ref):
    h = _rms(m_ref[0], g_ref[...]).astype(BF16)
    k_ref[0] = _mm(h, wk_ref[...], preferred_element_type=F32).astype(BF16)
    v_ref[0] = _mm(h, wv_ref[...], preferred_element_type=F32).astype(BF16)


def _memkv(mem, g, wk, wv):
    B, M, D = mem.shape
    blk = pl.BlockSpec((1, M, D), lambda b: (b, 0, 0))
    return pl.pallas_call(
        _memkv_kernel,
        grid=(B,),
        in_specs=[blk, _const_spec((1, D)), _const_spec(wk.shape), _const_spec(wv.shape)],
        out_specs=(blk, blk),
        out_shape=(jax.ShapeDtypeStruct(mem.shape, BF16), jax.ShapeDtypeStruct(mem.shape, BF16)),
        compiler_params=_cparams("parallel"),
    )(mem, g.reshape(1, D), wk, wv)


def _xattn_kernel(x_ref, pre_ref, wq_ref, k_ref, v_ref, wo_ref, post_ref, o_ref):
    x = x_ref[0]
    h = _rms(x, pre_ref[...]).astype(BF16)
    q = (_mm(h, wq_ref[...], preferred_element_type=F32) * (MEM_HEAD_DIM ** -0.5)).astype(BF16)
    heads = []
    for hd in range(MEM_HEADS):
        cols = slice(hd * MEM_HEAD_DIM, (hd + 1) * MEM_HEAD_DIM)
        s = _mmg(q[:, cols], k_ref[0, :, cols], _NT, preferred_element_type=F32)
        p = jnp.exp(s - jnp.max(s, axis=-1, keepdims=True))
        den = jnp.sum(p, axis=-1, keepdims=True)
        oh = _mm(p.astype(BF16), v_ref[0, :, cols], preferred_element_type=F32) / den
        heads.append(oh.astype(BF16))
    o = jnp.concatenate(heads, axis=1)
    y = _mm(o, wo_ref[...], preferred_element_type=F32)
    o_ref[0] = x + _rms(y, post_ref[...])


def _xattn(x, pre_g, wq, km, vm, wo, post_g):
    B, T, D = x.shape
    M = km.shape[1]
    tm = min(TM, T)
    row = pl.BlockSpec((1, tm, D), lambda b, i: (b, i, 0))
    kv = pl.BlockSpec((1, M, D), lambda b, i: (b, 0, 0))
    return pl.pallas_call(
        _xattn_kernel,
        grid=(B, T // tm),
        in_specs=[row, _const_spec((1, D)), _const_spec(wq.shape), kv, kv, _const_spec(wo.shape),
                  _const_spec((1, D))],
        out_specs=row,
        out_shape=jax.ShapeDtypeStruct(x.shape, F32),
        compiler_params=_cparams("parallel", "parallel"),
    )(x, pre_g.reshape(1, D), wq, km, vm, wo, post_g.reshape(1, D))


def _hybrid_mixer(x, pre_g, w_in, cmp_pe, cmp_w1, cmp_w2, nsa_out_g, ret_gn_g, w_out, post_g):
    q_hm, cmp, ks, vst, kw, vwt, gt, qr, kr, vr, gr = _inproj(x, pre_g, _inproj_weight(w_in))
    c_rm, c_t = _compress(cmp, *_compress_weights(cmp_pe, cmp_w1, cmp_w2))
    oc, sel, flags = _select(q_hm, c_rm, c_t)
    o_nsa = _attn(q_hm, ks, vst, kw, vwt, sel, flags, oc, gt, _attn_gain(nsa_out_g))
    o_ret = _retention(qr, kr, vr, gr, ret_gn_g)
    wn, wr = _outproj_weights(w_out)
    return _outproj(x, o_nsa, o_ret, wn, wr, post_g)


def kernel(x, mem, ffn1_pre_g, ffn1_w_gate, ffn1_w_up, ffn1_w_down, ffn1_post_g, mix_pre_g, w_in, cmp_pe, cmp_w1, cmp_w2, nsa_out_g, ret_gn_g, w_out, mix_post_g, xa_pre_g, xa_mem_g, xa_wq, xa_wk, xa_wv, xa_wo, xa_post_g, ffn2_pre_g, ffn2_w_gate, ffn2_w_up, ffn2_w_down, ffn2_post_g):
    depth = w_in.shape[0]
    bf = lambda w: w.astype(BF16)
    for l in range(depth):
        x = _ffn(x, ffn1_pre_g[l], bf(ffn1_w_gate[l]), bf(ffn1_w_up[l]), bf(ffn1_w_down[l]),
                 ffn1_post_g[l])
        x = _hybrid_mixer(x, mix_pre_g[l], w_in[l], cmp_pe[l], cmp_w1[l], cmp_w2[l], nsa_out_g[l],
                          ret_gn_g[l], w_out[l], mix_post_g[l])
        km, vm = _memkv(mem, xa_mem_g[l], bf(xa_wk[l]), bf(xa_wv[l]))
        x = _xattn(x, xa_pre_g[l], bf(xa_wq[l]), km, vm, bf(xa_wo[l]), xa_post_g[l])
        x = _ffn(x, ffn2_pre_g[l], bf(ffn2_w_gate[l]), bf(ffn2_w_up[l]), bf(ffn2_w_down[l]),
                 ffn2_post_g[l])
    return x
```

```python
import functools

import numpy as np
import jax
import jax.numpy as jnp
from jax import lax
from jax.experimental import pallas as pl
from jax.experimental.pallas import tpu as pltpu

F32 = jnp.float32
BF16 = jnp.bfloat16

D_MODEL = 1024
NSA_HEADS = 8
NSA_HEAD_DIM = 64
NSA_KV_HEADS = 2
NSA_GROUP = NSA_HEADS // NSA_KV_HEADS
NSA_WIDTH = NSA_HEADS * NSA_HEAD_DIM
KV_WIDTH = NSA_KV_HEADS * NSA_HEAD_DIM
CMP_LEN = 32
CMP_STRIDE = 16
CMP_HIDDEN = 256
SLC_LEN = 64
SLC_TOPN = 16
WINDOW = 512
RET_HEADS = 4
RET_HEAD_DIM = 128
RET_WIDTH = RET_HEADS * RET_HEAD_DIM
RET_CHUNK = 128
MEM_HEADS = 4
MEM_HEAD_DIM = D_MODEL // MEM_HEADS
D_FF = 2816
EPS = 1e-6
NEG = -1e30
TAKEN = -3e38
GATE_ROWS = 32

IN_SIZES = (NSA_WIDTH, KV_WIDTH, KV_WIDTH, KV_WIDTH, KV_WIDTH, KV_WIDTH, KV_WIDTH,
            NSA_HEADS * 3, RET_WIDTH, RET_WIDTH, RET_WIDTH, RET_WIDTH)
IN_OFFSETS = tuple(int(o) for o in np.cumsum(IN_SIZES)[:-1])

LANES = 128
QPAD = NSA_HEADS * LANES
LOG2E = 1.4426950408889634
ALIBI2 = tuple(float(2.0 ** (-8.0 * i / NSA_HEADS)) * LOG2E for i in range(1, NSA_HEADS + 1))
ONES_ROWS = 16
VT_ROWS = KV_WIDTH + ONES_ROWS
CH = 256

TM = 512
TQ = 128
TK = 512
WKEYS = WINDOW + TQ
VMEM_LIMIT = 56 * 1024 * 1024

_NT = (((1,), (1,)), ((), ()))
_TN = (((0,), (0,)), ((), ()))


def _mm(a, b, preferred_element_type=F32):
    return jnp.dot(a, b, preferred_element_type=preferred_element_type)


def _mmg(a, b, dims, preferred_element_type=F32):
    return lax.dot_general(a, b, dims, preferred_element_type=preferred_element_type)


def _cparams(*sem):
    return pltpu.CompilerParams(dimension_semantics=sem, vmem_limit_bytes=VMEM_LIMIT)


def _rms(x, g):
    return x * lax.rsqrt(jnp.mean(x * x, axis=-1, keepdims=True) + EPS) * g


def _const_spec(shape):
    nd = len(shape)
    return pl.BlockSpec(shape, lambda *_: (0,) * nd, pipeline_mode=pl.Buffered(1))


def _ffn_kernel(x_ref, pre_ref, wg_ref, wu_ref, wd_ref, post_ref, o_ref):
    x = x_ref[0]
    h = _rms(x, pre_ref[...]).astype(BF16)
    g = _mm(h, wg_ref[...], preferred_element_type=F32)
    u = _mm(h, wu_ref[...], preferred_element_type=F32)
    a = (g * jax.nn.sigmoid(g) * u).astype(BF16)
    y = _mm(a, wd_ref[...], preferred_element_type=F32)
    o_ref[0] = x + 0.5 * _rms(y, post_ref[...])


def _ffn(x, pre_g, wg, wu, wd, post_g):
    B, T, D = x.shape
    tm = min(TM, T)
    return pl.pallas_call(
        _ffn_kernel,
        grid=(B, T // tm),
        in_specs=[pl.BlockSpec((1, tm, D), lambda b, i: (b, i, 0)),
                  _const_spec((1, D)), _const_spec(wg.shape), _const_spec(wu.shape),
                  _const_spec(wd.shape), _const_spec((1, D))],
        out_specs=pl.BlockSpec((1, tm, D), lambda b, i: (b, i, 0)),
        out_shape=jax.ShapeDtypeStruct(x.shape, F32),
        compiler_params=_cparams("parallel", "parallel"),
    )(x, pre_g.reshape(1, D), wg, wu, wd, post_g.reshape(1, D))


_C_Q = 0
_C_CMP = _C_Q + QPAD
_C_KS = _C_CMP + 2 * KV_WIDTH
_C_VS = _C_KS + KV_WIDTH
_C_KW = _C_VS + KV_WIDTH
_C_VW = _C_KW + KV_WIDTH
_C_GATE = _C_VW + KV_WIDTH
_C_QR = _C_GATE + LANES
_C_KR = _C_QR + RET_WIDTH
_C_VR = _C_KR + RET_WIDTH
_C_GR = _C_VR + RET_WIDTH
_C_END = _C_GR + RET_WIDTH


def _inproj_kernel(x_ref, pre_ref, w_ref, q_ref, cmp_ref, ks_ref, vst_ref, kw_ref, vwt_ref,
                   gt_ref, qr_ref, kr_ref, vr_ref, gr_ref):
    tm = x_ref.shape[1]
    h = _rms(x_ref[0], pre_ref[...]).astype(BF16)

    def proj(lo, hi):
        return _mm(h, w_ref[:, lo:hi], preferred_element_type=F32)

    q = proj(_C_Q, _C_CMP)
    for j in range(tm // TQ):
        for hh in range(NSA_HEADS):
            q_ref[0, j, :, hh * TQ:(hh + 1) * TQ] = (
                q[j * TQ:(j + 1) * TQ, hh * LANES:(hh + 1) * LANES].T.astype(BF16))
    c = proj(_C_CMP, _C_KS)
    cmp_ref[0, 0] = c[:, :KV_WIDTH].astype(BF16)
    cmp_ref[0, 1] = c[:, KV_WIDTH:].astype(BF16)
    ks_ref[0] = proj(_C_KS, _C_VS).astype(BF16)
    kw_ref[0] = proj(_C_KW, _C_VW).astype(BF16)
    vs = proj(_C_VS, _C_KW)
    vw = proj(_C_VW, _C_GATE)
    ones = jnp.ones((ONES_ROWS, LANES), BF16)
    for j in range(tm // LANES):
        vst_ref[0, j, :KV_WIDTH, :] = vs[j * LANES:(j + 1) * LANES, :].T.astype(BF16)
        vwt_ref[0, j, :KV_WIDTH, :] = vw[j * LANES:(j + 1) * LANES, :].T.astype(BF16)
        vst_ref[0, j, KV_WIDTH:, :] = ones
        vwt_ref[0, j, KV_WIDTH:, :] = ones
    gt_ref[0] = proj(_C_GATE, _C_QR).T[:GATE_ROWS, :]
    qr_ref[0] = proj(_C_QR, _C_KR).astype(BF16)
    kr_ref[0] = proj(_C_KR, _C_VR).astype(BF16)
    vr_ref[0] = proj(_C_VR, _C_GR).astype(BF16)
    gr_ref[0] = proj(_C_GR, _C_END)


def _inproj_weight(w_in):
    parts = jnp.split(w_in, IN_OFFSETS, axis=-1)
    q_n, kc, vc, ks, vs, kw, vw, gates, q_r, k_r, v_r, g_r = parts
    scale = NSA_HEAD_DIM ** -0.5 * LOG2E
    qcols = []
    for hh in range(NSA_HEADS):
        g = hh // NSA_GROUP
        blk = q_n[:, hh * NSA_HEAD_DIM:(hh + 1) * NSA_HEAD_DIM] * scale
        z = jnp.zeros_like(blk)
        qcols += [blk, z] if g == 0 else [z, blk]
    gates_p = jnp.pad(gates, ((0, 0), (0, LANES - gates.shape[1])))
    w = jnp.concatenate(qcols + [kc, vc, ks, vs, kw, vw, gates_p, q_r, k_r, v_r, g_r], axis=1)
    assert w.shape[1] == _C_END
    return w.astype(BF16)


def _inproj(x, pre_g, w_all):
    B, T, D = x.shape
    tm = min(TM, T)
    nb = tm // LANES
    row = lambda w: pl.BlockSpec((1, tm, w), lambda b, i: (b, i, 0))
    out_shape = (
        jax.ShapeDtypeStruct((B, T // TQ, LANES, NSA_HEADS * TQ), BF16),
        jax.ShapeDtypeStruct((B, 2, T, KV_WIDTH), BF16),
        jax.ShapeDtypeStruct((B, T, KV_WIDTH), BF16),
        jax.ShapeDtypeStruct((B, T // LANES, VT_ROWS, LANES), BF16),
        jax.ShapeDtypeStruct((B, T, KV_WIDTH), BF16),
        jax.ShapeDtypeStruct((B, T // LANES, VT_ROWS, LANES), BF16),
        jax.ShapeDtypeStruct((B, GATE_ROWS, T), F32),
        jax.ShapeDtypeStruct((B, T, RET_WIDTH), BF16),
        jax.ShapeDtypeStruct((B, T, RET_WIDTH), BF16),
        jax.ShapeDtypeStruct((B, T, RET_WIDTH), BF16),
        jax.ShapeDtypeStruct((B, T, RET_WIDTH), F32),
    )
    vt_spec = pl.BlockSpec((1, nb, VT_ROWS, LANES), lambda b, i: (b, i, 0, 0))
    out_specs = (
        pl.BlockSpec((1, tm // TQ, LANES, NSA_HEADS * TQ), lambda b, i: (b, i, 0, 0)),
        pl.BlockSpec((1, 2, tm, KV_WIDTH), lambda b, i: (b, 0, i, 0)),
        row(KV_WIDTH), vt_spec, row(KV_WIDTH), vt_spec,
        pl.BlockSpec((1, GATE_ROWS, tm), lambda b, i: (b, 0, i)),
        row(RET_WIDTH), row(RET_WIDTH), row(RET_WIDTH), row(RET_WIDTH),
    )
    return pl.pallas_call(
        _inproj_kernel,
        grid=(B, T // tm),
        in_specs=[pl.BlockSpec((1, tm, D), lambda b, i: (b, i, 0)),
                  _const_spec((1, D)), _const_spec(w_all.shape)],
        out_specs=out_specs,
        out_shape=out_shape,
        compiler_params=_cparams("parallel", "parallel"),
    )(x, pre_g.reshape(1, D), w_all)


def _compress_kernel(a_ref, pe_ref, w1_ref, w2_ref, rm_ref, t_ref, hi_scr):
    ncp = a_ref.shape[2]
    a = a_ref[0, 0].astype(F32)
    a_lo = (a + pe_ref[0, 0:1, :]).astype(BF16)
    a_hi = (a + pe_ref[0, 1:2, :]).astype(BF16)
    c = jnp.zeros((ncp, KV_WIDTH), F32)
    hi_scr[pl.ds(ncp, 8), :] = jnp.zeros((8, CMP_HIDDEN), F32)
    for g in range(NSA_KV_HEADS):
        p_lo = _mm(a_lo, w1_ref[0, 2 * g], preferred_element_type=F32)
        hi_scr[pl.ds(0, ncp), :] = _mm(a_hi, w1_ref[0, 2 * g + 1], preferred_element_type=F32)
        hdn = p_lo + hi_scr[pl.ds(1, ncp), :]
        hdn = (hdn * jax.nn.sigmoid(hdn)).astype(BF16)
        c = c + _mm(hdn, w2_ref[0, g], preferred_element_type=F32)
    rows = lax.broadcasted_iota(jnp.int32, c.shape, 0)
    c = jnp.where(rows < ncp - 1, c, 0.0)
    rm_ref[0, 0] = c.astype(BF16)
    for j in range(ncp // CH):
        t_ref[0, 0, j] = c[j * CH:(j + 1) * CH, :].T.astype(BF16)


def _compress_weights(cmp_pe, cmp_w1, cmp_w2):
    half = CMP_LEN // 2
    pe = cmp_pe.reshape(2, 2, half, 1, NSA_HEAD_DIM)
    pe = jnp.broadcast_to(pe, (2, 2, half, NSA_KV_HEADS, NSA_HEAD_DIM)).reshape(2, 2, half * KV_WIDTH)
    w1 = cmp_w1.reshape(2, 2, half, NSA_HEAD_DIM, CMP_HIDDEN)
    w1e = []
    for g in range(NSA_KV_HEADS):
        for hf in range(2):
            z = jnp.zeros((2, half, NSA_KV_HEADS, NSA_HEAD_DIM, CMP_HIDDEN), F32)
            z = z.at[:, :, g].set(w1[:, hf])
            w1e.append(z.reshape(2, half * KV_WIDTH, CMP_HIDDEN))
    w1e = jnp.stack(w1e, axis=1).astype(BF16)
    w2e = []
    for g in range(NSA_KV_HEADS):
        z = jnp.zeros((2, CMP_HIDDEN, NSA_KV_HEADS, NSA_HEAD_DIM), F32)
        z = z.at[:, :, g].set(cmp_w2)
        w2e.append(z.reshape(2, CMP_HIDDEN, KV_WIDTH))
    w2e = jnp.stack(w2e, axis=1).astype(BF16)
    return pe, w1e, w2e


def _compress(cmp, pe, w1e, w2e):
    B, _, T, _ = cmp.shape
    ncp = T // CMP_STRIDE
    a = cmp.reshape(B, 2, ncp, CMP_STRIDE * KV_WIDTH)
    kw = CMP_STRIDE * KV_WIDTH
    return pl.pallas_call(
        _compress_kernel,
        grid=(B, 2),
        in_specs=[pl.BlockSpec((1, 1, ncp, kw), lambda b, k: (b, k, 0, 0)),
                  pl.BlockSpec((1, 2, kw), lambda b, k: (k, 0, 0)),
                  pl.BlockSpec((1, 4, kw, CMP_HIDDEN), lambda b, k: (k, 0, 0, 0)),
                  pl.BlockSpec((1, 2, CMP_HIDDEN, KV_WIDTH), lambda b, k: (k, 0, 0, 0))],
        out_specs=(pl.BlockSpec((1, 1, ncp, KV_WIDTH), lambda b, k: (b, k, 0, 0)),
                   pl.BlockSpec((1, 1, ncp // CH, KV_WIDTH, CH), lambda b, k: (b, k, 0, 0, 0))),
        out_shape=(jax.ShapeDtypeStruct((B, 2, ncp, KV_WIDTH), BF16),
                   jax.ShapeDtypeStruct((B, 2, ncp // CH, KV_WIDTH, CH), BF16)),
        scratch_shapes=[pltpu.VMEM((ncp + 8, CMP_HIDDEN), F32)],
        compiler_params=_cparams("parallel", "parallel"),
    )(a, pe, w1e, w2e)


def _select_kernel(q_ref, kc_ref, vct_ref, ovl_ref, tab_ref, grp_ref, oc_ref, sel_ref, flag_ref,
                   s_scr, p_scr, imp_scr, oc_scr):
    tq = TQ
    nsel = ovl_ref.shape[1]
    q0 = pl.program_id(1) * tq
    qt = q_ref[0, 0]
    span = CH * CMP_STRIDE
    nch = (q0 + tq - CMP_LEN) // span + 1
    nfull = jnp.maximum((q0 - (span + CMP_LEN - CMP_STRIDE - 1)) // span + 1, 0)
    mrel = (lax.broadcasted_iota(jnp.int32, (CH, tq), 0) * CMP_STRIDE + (CMP_LEN - 1)
            - lax.broadcasted_iota(jnp.int32, (CH, tq), 1))
    heads = [slice(hh * tq, (hh + 1) * tq) for hh in range(NSA_HEADS)]

    def chunk_rows(c):
        return pl.ds(pl.multiple_of(c * CH, CH), CH)

    def scores(masked):
        def body(c, ms):
            rows = chunk_rows(c)
            s_c = _mm(kc_ref[0, 0, rows, :], qt, preferred_element_type=F32)
            ok = mrel <= q0 - c * span
            out = []
            for hh, sl in enumerate(heads):
                s = s_c[:, sl] + tab_ref[hh, rows, :]
                if masked:
                    s = jnp.where(ok, s, NEG)
                s_scr[rows, sl] = s
                out.append(jnp.maximum(ms[hh], jnp.max(s, axis=0, keepdims=True)))
            return tuple(out)
        return body

    m0 = tuple(jnp.full((1, tq), NEG, F32) for _ in heads)
    ms = lax.fori_loop(0, nfull, scores(False), m0)
    ms = lax.fori_loop(nfull, nch, scores(True), ms)

    def probs(masked):
        def body(c, dens):
            rows = chunk_rows(c)
            ok = mrel <= q0 - c * span
            out = []
            for hh, sl in enumerate(heads):
                p = jnp.exp2(s_scr[rows, sl] - ms[hh])
                if masked:
                    p = jnp.where(ok, p, 0.0)
                s_scr[rows, sl] = p
                out.append(dens[hh] + jnp.sum(p, axis=0, keepdims=True))
            return tuple(out)
        return body

    d0 = tuple(jnp.zeros((1, tq), F32) for _ in heads)
    dens = lax.fori_loop(0, nfull, probs(False), d0)
    dens = lax.fori_loop(nfull, nch, probs(True), dens)
    rden = [1.0 / jnp.maximum(d, 1e-30) for d in dens]

    imp_scr[...] = jnp.zeros_like(imp_scr)
    oc_scr[...] = jnp.zeros_like(oc_scr)

    def outputs(c, carry):
        rows = chunk_rows(c)
        for g in range(NSA_KV_HEADS):
            psum = jnp.zeros((CH, tq), F32)
            for r in range(NSA_GROUP):
                hh = g * NSA_GROUP + r
                pn = s_scr[rows, heads[hh]] * rden[hh]
                psum = psum + pn
                p_scr[:, heads[hh]] = pn.astype(BF16)
            p_hi = psum.astype(BF16)
            p_lo = (psum - p_hi.astype(F32)).astype(BF16)
            imp_scr[g] += (_mm(ovl_ref[c], p_hi, preferred_element_type=F32)
                           + _mm(ovl_ref[c], p_lo, preferred_element_type=F32))
        oc_scr[...] += _mm(vct_ref[0, 0, c], p_scr[...], preferred_element_type=F32)
        return carry

    lax.fori_loop(0, nch, outputs, 0)
    for hh, sl in enumerate(heads):
        oc_ref[0, 0, hh] = oc_scr[:, sl]

    n_top = min(SLC_TOPN, nsel)
    n_seen = (q0 + tq - 1) // SLC_LEN + 1
    sizes = sorted({r for r in (nsel * i // 8 for i in range(1, 9)) if r >= n_top and r % 8 == 0})

    def select(rows):
        jb = lax.broadcasted_iota(jnp.int32, (rows, tq), 0)
        cur = (q0 + lax.broadcasted_iota(jnp.int32, (rows, tq), 1)) // SLC_LEN
        forced = (jb == 0) | (jb == cur) | (jb == cur - 1)
        cur1 = (q0 + lax.broadcasted_iota(jnp.int32, (1, tq), 1)) // SLC_LEN
        quota = n_top - (1 + (cur1 >= 1).astype(jnp.int32) + (cur1 >= 2).astype(jnp.int32))
        imps = tuple(jnp.where(forced, TAKEN, jnp.where(jb <= cur, imp_scr[g, pl.ds(0, rows), :], -1.0))
                     for g in range(NSA_KV_HEADS))

        def pick(i, imps):
            out = []
            for imp in imps:
                best = jnp.max(imp, axis=0, keepdims=True)
                first = jnp.min(jnp.where(imp == best, jb, rows), axis=0, keepdims=True)
                first = jnp.where(i < quota, first, -1)
                out.append(jnp.where(jb == first, TAKEN, imp))
            return tuple(out)

        rounds = n_top - jnp.where(q0 >= 2 * SLC_LEN, 3, 1)
        imps = lax.fori_loop(0, rounds, pick, imps)
        ones = jnp.ones((ONES_ROWS, tq), BF16)
        cnt = jnp.zeros((ONES_ROWS, rows), F32)
        for g in range(NSA_KV_HEADS):
            taken = imps[g] < -2.0
            sel_ref[0, g, pl.ds(0, rows), :] = jnp.where(taken, 0.0, NEG)
            if rows < nsel:
                sel_ref[0, g, pl.ds(rows, nsel - rows), :] = jnp.full((nsel - rows, tq), NEG, F32)
            cnt = cnt + _mmg(ones, jnp.where(taken, 1.0, 0.0).astype(BF16), _NT,
                             preferred_element_type=F32)
        act = _mm(jnp.minimum(cnt, 1.0).astype(BF16), grp_ref[pl.ds(0, rows), :],
                  preferred_element_type=F32)
        flag_ref[0, 0] = (act[0:1, :] > 0.0).astype(jnp.int32)

    lo = 0
    for rows in sizes:
        cond = n_seen > lo
        if rows < nsel:
            cond = cond & (n_seen <= rows)
        pl.when(cond)(functools.partial(select, rows))
        lo = rows


def _overlap_t(T):
    ncp = T // CMP_STRIDE
    nc = (T - CMP_LEN) // CMP_STRIDE + 1
    nsel = T // SLC_LEN
    c_start = np.arange(nc) * CMP_STRIDE
    c_end = c_start + CMP_LEN - 1
    s_start = np.arange(nsel) * SLC_LEN
    s_end = s_start + SLC_LEN - 1
    ov = np.clip(np.minimum(c_end[:, None], s_end[None, :])
                 - np.maximum(c_start[:, None], s_start[None, :]) + 1, 0, None) / CMP_LEN
    out = np.zeros((nsel, ncp), np.float32)
    out[:, :nc] = ov.T
    out = out.reshape(nsel, ncp // CH, CH).transpose(1, 0, 2)
    return jnp.asarray(out, dtype=BF16)


def _select_tables(T):
    ncp = T // CMP_STRIDE
    nsel = T // SLC_LEN
    nkt = T // min(TK, T)
    n = np.arange(ncp, dtype=np.float64)[:, None]
    l = np.arange(TQ, dtype=np.float64)[None, :]
    dist = l - (n * CMP_STRIDE + (CMP_LEN - 1) / 2.0)
    tab = np.stack([-a * dist for a in ALIBI2]).astype(np.float32)
    grp = (np.arange(nsel)[:, None] // (nsel // nkt) == np.arange(nkt)[None, :]).astype(np.float32)
    grp[0, :] = 0.0
    return jnp.asarray(tab), jnp.asarray(grp, dtype=BF16)


def _select(q_hm, c_rm, c_t):
    B, T = q_hm.shape[0], q_hm.shape[1] * TQ
    ncp = T // CMP_STRIDE
    nsel = T // SLC_LEN
    nqt = T // TQ
    nkt = T // min(TK, T)
    tab, grp = _select_tables(T)
    return pl.pallas_call(
        _select_kernel,
        grid=(B, nqt),
        in_specs=[pl.BlockSpec((1, 1, LANES, NSA_HEADS * TQ), lambda b, i: (b, i, 0, 0)),
                  pl.BlockSpec((1, 1, ncp, KV_WIDTH), lambda b, i: (b, 0, 0, 0)),
                  pl.BlockSpec((1, 1, ncp // CH, KV_WIDTH, CH), lambda b, i: (b, 1, 0, 0, 0)),
                  _const_spec((ncp // CH, nsel, CH)), _const_spec(tab.shape), _const_spec(grp.shape)],
        out_specs=(pl.BlockSpec((1, 1, NSA_HEADS, LANES, TQ), lambda b, i: (b, i, 0, 0, 0)),
                   pl.BlockSpec((1, NSA_KV_HEADS, nsel, TQ), lambda b, i: (b, 0, 0, i)),
                   pl.BlockSpec((1, 1, 1, nkt), lambda b, i: (b, i, 0, 0))),
        out_shape=(jax.ShapeDtypeStruct((B, nqt, NSA_HEADS, LANES, TQ), F32),
                   jax.ShapeDtypeStruct((B, NSA_KV_HEADS, nsel, T), F32),
                   jax.ShapeDtypeStruct((B, nqt, 1, nkt), jnp.int32)),
        scratch_shapes=[pltpu.VMEM((ncp, NSA_HEADS * TQ), F32),
                        pltpu.VMEM((CH, NSA_HEADS * TQ), BF16),
                        pltpu.VMEM((NSA_KV_HEADS, nsel, TQ), F32),
                        pltpu.VMEM((KV_WIDTH, NSA_HEADS * TQ), F32)],
        compiler_params=_cparams("parallel", "arbitrary"),
    )(q_hm, c_rm, c_t, _overlap_t(T), tab, grp)


def _attn_kernel(flags_ref, q_ref, ks_ref, vst_ref, kw_ref, vwt_ref, sel_ref, oc_ref, gt_ref,
                 gain_ref, kf_ref, kh_ref, qf_ref, wb_ref, o_ref, acc_scr, p_scr, m_scr, alpha_scr,
                 pw_scr, ow_scr, qf_scr):
    tq = TQ
    T = ks_ref.shape[1]
    qi = pl.program_id(1)
    q0 = qi * tq
    tk = kf_ref.shape[0]
    wk = wb_ref.shape[1]
    nkt = T // tk
    bpt = tk // SLC_LEN
    vpt = tk // LANES
    assert bpt == 8
    qt = q_ref[0, 0]
    heads = [slice(hh * tq, (hh + 1) * tq) for hh in range(NSA_HEADS)]

    qf_scr[pl.ds(0, LANES), :] = qt
    qf_scr[pl.ds(LANES, LANES), :] = qf_ref[...]
    row8 = lax.broadcasted_iota(jnp.int32, (8, tq), 0)

    def digits3(v):
        d1 = v.astype(BF16).astype(F32)
        d2 = (v - d1).astype(BF16).astype(F32)
        return d1, d2, v - d1 - d2

    def tile(kt, rows, first, head=False):
        k0 = kt * tk
        shift = (q0 - k0).astype(F32)
        for g in range(NSA_KV_HEADS):
            sb = sel_ref[0, g, pl.ds(pl.multiple_of(kt * bpt, bpt), bpt), :]
            for r in range(NSA_GROUP):
                hh = g * NSA_GROUP + r
                d = digits3(jnp.full((8, tq), -ALIBI2[hh], F32) * shift)
                low = jnp.where(row8 == 0, d[0], jnp.where(row8 == 1, d[1], jnp.where(row8 == 2, d[2], 0.0)))
                if head:
                    e = digits3(jnp.full((8, tq), -ALIBI2[hh], F32) * q0.astype(F32))
                    low = jnp.where(row8 == 3, e[0], jnp.where(row8 == 4, e[1], jnp.where(row8 == 5, e[2], low)))
                qf_scr[pl.ds(LANES, 16), heads[hh]] = jnp.concatenate([sb, low], axis=0).astype(BF16)
                if head:
                    qf_scr[pl.ds(LANES + 16, 16), heads[hh]] = jnp.concatenate(
                        [sel_ref[0, g, pl.ds(0, 8), :], jnp.zeros((8, tq), F32)], axis=0).astype(BF16)
        lhs = jnp.concatenate([ks_ref[0, pl.ds(pl.multiple_of(k0, tk), rows), :],
                               kf_ref[pl.ds(0, rows), :]], axis=1)
        vts = [vst_ref[0, kt * vpt + j] for j in range(rows // LANES)]
        if head:
            lhs = jnp.concatenate(
                [jnp.concatenate([ks_ref[0, pl.ds(0, LANES), :], kh_ref[...]], axis=1), lhs], axis=0)
            vts = [vst_ref[0, 0]] + vts
            rows = rows + LANES
        s_all = _mm(lhs, qf_scr[...], preferred_element_type=F32)
        vt = jnp.concatenate(vts, axis=1)
        if first:
            c_i = lax.broadcasted_iota(jnp.int32, (rows, tq), 0)
            l_i = lax.broadcasted_iota(jnp.int32, (rows, tq), 1)
            if head:
                causal = (c_i < LANES) | (c_i + (k0 - q0 - LANES) <= l_i)
            else:
                causal = c_i + (k0 - q0) <= l_i
        for hh, sl in enumerate(heads):
            s = s_all[:, sl]
            if first:
                s = jnp.where(causal, s, NEG)
                m_new = jnp.max(s, axis=0, keepdims=True)
            else:
                m_old = m_scr[:, sl]
                m_new = jnp.maximum(m_old, jnp.max(s, axis=0, keepdims=True))
                alpha_scr[:, sl] = jnp.exp2(m_old - m_new)
            m_scr[:, sl] = m_new
            p_scr[pl.ds(0, rows), sl] = jnp.exp2(s - m_new).astype(BF16)
        pv = _mm(vt, p_scr[pl.ds(0, rows), :], preferred_element_type=F32)
        if first:
            acc_scr[...] = pv
        else:
            acc_scr[...] = acc_scr[...] * alpha_scr[...] + pv

    kt_diag = q0 // tk
    fbase = (pl.program_id(0) * pl.num_programs(1) + qi) * nkt
    with_head = (kt_diag > 0) & (flags_ref[fbase] == 0)
    for jj in range(tk // tq):
        here = qi % (tk // tq) == jj
        pl.when(here & with_head)(functools.partial(tile, kt_diag, (jj + 1) * tq, True, True))
        pl.when(here & jnp.logical_not(with_head))(
            functools.partial(tile, kt_diag, (jj + 1) * tq, True, False))

    def body(kt, carry):
        @pl.when(flags_ref[fbase + kt] != 0)
        def _():
            tile(kt, tk, False)
        return carry

    lax.fori_loop(0, kt_diag, body, 0)

    def window(ws, general):
        sw_all = _mm(kw_ref[0, pl.ds(pl.multiple_of(ws, LANES), wk), :], qt,
                     preferred_element_type=F32)
        vwt = jnp.concatenate([vwt_ref[0, ws // LANES + j] for j in range(wk // LANES)], axis=1)
        if general:
            kpos = ws + lax.broadcasted_iota(jnp.int32, (wk, tq), 0)
            dist = q0 + lax.broadcasted_iota(jnp.int32, (wk, tq), 1) - kpos
            wmask = (dist >= 0) & (dist < WINDOW)
            distf = dist.astype(F32)
        for hh, sl in enumerate(heads):
            if general:
                s = jnp.where(wmask, sw_all[:, sl] - ALIBI2[hh] * distf, NEG)
            else:
                s = sw_all[:, sl] + wb_ref[hh]
            pw_scr[:, sl] = jnp.exp2(s - jnp.max(s, axis=0, keepdims=True)).astype(BF16)
        ow_scr[...] = _mm(vwt, pw_scr[...], preferred_element_type=F32)

    @pl.when(q0 >= WINDOW)
    def _():
        window(q0 - WINDOW, False)

    @pl.when(q0 < WINDOW)
    def _():
        window(q0 * 0, True)

    gs = jax.nn.sigmoid(gt_ref[0])
    outs = []
    ssq = jnp.zeros((1, tq), F32)
    for hh in range(NSA_HEADS):
        sl = heads[hh]
        ch = pl.ds((hh // NSA_GROUP) * NSA_HEAD_DIM, NSA_HEAD_DIM)
        o_s = acc_scr[ch, sl] / jnp.maximum(acc_scr[KV_WIDTH:KV_WIDTH + 1, sl], 1e-30)
        o_w = ow_scr[ch, sl] / jnp.maximum(ow_scr[KV_WIDTH:KV_WIDTH + 1, sl], 1e-30)
        o = (gs[3 * hh:3 * hh + 1, :] * oc_ref[0, 0, hh, ch, :] + gs[3 * hh + 1:3 * hh + 2, :] * o_s
             + gs[3 * hh + 2:3 * hh + 3, :] * o_w)
        ssq = ssq + jnp.sum(o * o, axis=0, keepdims=True)
        outs.append(o)
    rinv = lax.rsqrt(ssq / NSA_WIDTH + EPS)
    for j in range(NSA_HEADS // 2):
        pair = jnp.concatenate([outs[2 * j] * rinv * gain_ref[2 * j],
                                outs[2 * j + 1] * rinv * gain_ref[2 * j + 1]], axis=0)
        o_ref[0, :, j * LANES:(j + 1) * LANES] = pair.T.astype(BF16)


def _attn_gain(nsa_out_g):
    g = nsa_out_g.reshape(NSA_HEADS, NSA_HEAD_DIM)
    return jnp.broadcast_to(g[:, :, None], (NSA_HEADS, NSA_HEAD_DIM, TQ))


def _bf16_digits(x, n):
    out = []
    for _ in range(n):
        d = float(np.asarray(x, np.float32).astype(jnp.bfloat16).astype(np.float32))
        out.append(d)
        x = x - d
    return out


def _attn_tables(T):
    tk = min(TK, T)
    wk = min(WKEYS, T)
    c = np.arange(tk)
    kf = np.zeros((tk, LANES), np.float32)
    kf[c, c // SLC_LEN] = 1.0
    kf[:, 8:11] = 1.0
    kf[:, 32:35] = (c // 32)[:, None]
    kf[:, 35:38] = (c % 32)[:, None]
    ch = np.arange(LANES)
    kh = np.zeros((LANES, LANES), np.float32)
    kh[ch, 16 + ch // SLC_LEN] = 1.0
    kh[:, 11:14] = 1.0
    kh[:, 32:35] = (ch // 32)[:, None]
    kh[:, 35:38] = (ch % 32)[:, None]
    qf = np.zeros((LANES, NSA_HEADS * TQ), np.float32)
    for hh, a in enumerate(ALIBI2):
        d = _bf16_digits(a, 3)
        for i in range(3):
            qf[32 + i, hh * TQ:(hh + 1) * TQ] = 32.0 * d[i]
            qf[35 + i, hh * TQ:(hh + 1) * TQ] = d[i]
    l = np.arange(TQ, dtype=np.float64)[None, :]
    dist = WINDOW + l - np.arange(wk, dtype=np.float64)[:, None]
    inside = (dist >= 0) & (dist < WINDOW)
    wb = np.stack([np.where(inside, -a * dist, NEG) for a in ALIBI2]).astype(np.float32)
    return (jnp.asarray(kf, dtype=BF16), jnp.asarray(kh, dtype=BF16), jnp.asarray(qf, dtype=BF16),
            jnp.asarray(wb))


def _attn(q_hm, ks, vst, kw, vwt, sel, flags, oc, gt, gain_b):
    B, T = q_hm.shape[0], q_hm.shape[1] * TQ
    nsel = T // SLC_LEN
    nqt = T // TQ
    kf, kh, qf, wb = _attn_tables(T)
    tk, wk = kf.shape[0], wb.shape[1]
    once = dict(pipeline_mode=pl.Buffered(1))
    full_k = pl.BlockSpec((1, T, KV_WIDTH), lambda b, i, f: (b, 0, 0), **once)
    full_vt = pl.BlockSpec((1, T // LANES, VT_ROWS, LANES), lambda b, i, f: (b, 0, 0, 0), **once)
    const = lambda shape: pl.BlockSpec(shape, lambda b, i, f: (0,) * len(shape), **once)
    grid_spec = pltpu.PrefetchScalarGridSpec(
        num_scalar_prefetch=1,
        grid=(B, nqt),
        in_specs=[pl.BlockSpec((1, 1, LANES, NSA_HEADS * TQ), lambda b, i, f: (b, i, 0, 0)),
                  full_k, full_vt, full_k, full_vt,
                  pl.BlockSpec((1, NSA_KV_HEADS, nsel, TQ), lambda b, i, f: (b, 0, 0, i)),
                  pl.BlockSpec((1, 1, NSA_HEADS, LANES, TQ), lambda b, i, f: (b, i, 0, 0, 0)),
                  pl.BlockSpec((1, GATE_ROWS, TQ), lambda b, i, f: (b, 0, i)),
                  const((NSA_HEADS, NSA_HEAD_DIM, TQ)), const(kf.shape), const(kh.shape),
                  const(qf.shape), const(wb.shape)],
        out_specs=pl.BlockSpec((1, TQ, NSA_WIDTH), lambda b, i, f: (b, i, 0)),
        scratch_shapes=[pltpu.VMEM((VT_ROWS, NSA_HEADS * TQ), F32),
                        pltpu.VMEM((tk + LANES, NSA_HEADS * TQ), BF16),
                        pltpu.VMEM((1, NSA_HEADS * TQ), F32),
                        pltpu.VMEM((1, NSA_HEADS * TQ), F32),
                        pltpu.VMEM((wk, NSA_HEADS * TQ), BF16),
                        pltpu.VMEM((VT_ROWS, NSA_HEADS * TQ), F32),
                        pltpu.VMEM((2 * LANES, NSA_HEADS * TQ), BF16)])
    return pl.pallas_call(
        _attn_kernel,
        grid_spec=grid_spec,
        out_shape=jax.ShapeDtypeStruct((B, T, NSA_WIDTH), BF16),
        compiler_params=_cparams("parallel", "arbitrary"),
    )(flags.reshape(-1), q_hm, ks, vst, kw, vwt, sel, oc, gt, gain_b, kf, kh, qf, wb)


def _retention_kernel(q_ref, k_ref, v_ref, g_ref, dm_ref, qd_ref, kd_ref, gn_ref, o_ref, st_scr,
                      *, chunk_dec):
    rt = q_ref.shape[1]
    C, d = RET_CHUNK, RET_HEAD_DIM

    @pl.when(pl.program_id(1) == 0)
    def _():
        st_scr[...] = jnp.zeros_like(st_scr)

    for c in range(rt // C):
        rows = slice(c * C, (c + 1) * C)
        for h in range(RET_HEADS):
            cols = slice(h * d, (h + 1) * d)
            q = q_ref[0, rows, cols]
            k = k_ref[0, rows, cols]
            v = v_ref[0, rows, cols]
            state = st_scr[h]
            s = _mmg(q, k, _NT, preferred_element_type=F32) * dm_ref[h]
            inner = _mm(s.astype(BF16), v, preferred_element_type=F32)
            cross = _mm(q, state.astype(BF16), preferred_element_type=F32) * qd_ref[h]
            kd = (k.astype(F32) * kd_ref[h]).astype(BF16)
            st_scr[h] = state * chunk_dec[h] + _mmg(kd, v, _TN, preferred_element_type=F32)
            o = inner + cross
            mu = jnp.mean(o, axis=-1, keepdims=True)
            oc = o - mu
            var = jnp.mean(oc * oc, axis=-1, keepdims=True)
            o = oc * lax.rsqrt(var + EPS) * gn_ref[:, cols]
            gg = g_ref[0, rows, cols]
            o_ref[0, rows, cols] = (o * (gg * jax.nn.sigmoid(gg))).astype(BF16)


def _retention_consts():
    C, d = RET_CHUNK, RET_HEAD_DIM
    log_g = np.log(1.0 - np.exp2(-5.0 - np.arange(RET_HEADS, dtype=np.float64)))
    pos = np.arange(C, dtype=np.float64)
    diff = pos[:, None] - pos[None, :]
    scale = d ** -0.5
    dmask = np.where(diff >= 0, np.exp(np.maximum(diff, 0.0) * log_g[:, None, None]), 0.0) * scale
    q_dec = np.exp((pos + 1.0) * log_g[:, None])
    k_dec = np.exp((C - 1.0 - pos) * log_g[:, None]) * scale
    chunk_dec = tuple(float(x) for x in np.exp(C * log_g))
    bc = lambda a: jnp.asarray(np.broadcast_to(a[:, :, None], (RET_HEADS, C, d)).astype(np.float32))
    return jnp.asarray(dmask.astype(np.float32)), bc(q_dec), bc(k_dec), chunk_dec


def _retention(qr, kr, vr, gr, gn_gain):
    B, T, W = qr.shape
    rt = min(TM, T)
    dmask, qd, kd, chunk_dec = _retention_consts()
    row = pl.BlockSpec((1, rt, W), lambda b, i: (b, i, 0))
    cst = _const_spec((RET_HEADS, RET_CHUNK, RET_HEAD_DIM))
    return pl.pallas_call(
        functools.partial(_retention_kernel, chunk_dec=chunk_dec),
        grid=(B, T // rt),
        in_specs=[row, row, row, row, cst, cst, cst, _const_spec((1, W))],
        out_specs=row,
        out_shape=jax.ShapeDtypeStruct((B, T, W), BF16),
        scratch_shapes=[pltpu.VMEM((RET_HEADS, RET_HEAD_DIM, RET_HEAD_DIM), F32)],
        compiler_params=_cparams("parallel", "arbitrary"),
    )(qr, kr, vr, gr, dmask, qd, kd, gn_gain.reshape(1, W))


def _outproj_kernel(x_ref, on_ref, or_ref, wn_ref, wr_ref, post_ref, o_ref):
    y = (_mm(on_ref[0], wn_ref[...], preferred_element_type=F32)
         + _mm(or_ref[0], wr_ref[...], preferred_element_type=F32))
    o_ref[0] = x_ref[0] + _rms(y, post_ref[...])


def _outproj_weights(w_out):
    return w_out[:NSA_WIDTH].astype(BF16), w_out[NSA_WIDTH:].astype(BF16)


def _outproj(x, o_nsa, o_ret, wn, wr, post_g):
    B, T, D = x.shape
    tm = min(TM, T)
    row = lambda w: pl.BlockSpec((1, tm, w), lambda b, i: (b, i, 0))
    return pl.pallas_call(
        _outproj_kernel,
        grid=(B, T // tm),
        in_specs=[row(D), row(NSA_WIDTH), row(RET_WIDTH), _const_spec(wn.shape), _const_spec(wr.shape),
                  _const_spec((1, D))],
        out_specs=row(D),
        out_shape=jax.ShapeDtypeStruct(x.shape, F32),
        compiler_params=_cparams("parallel", "parallel"),
    )(x, o_nsa, o_ret, wn, wr, post_g.reshape(1, D))


def _memkv_kernel(m_ref, g_ref, wk_ref, wv_ref, k_ref, v_ref):
    h = _rms(m_ref[0], g_ref[...]).astype(BF16)
    k_ref[0] = _mm(h, wk_ref[...], preferred_element_type=F32).astype(BF16)
    v_ref[0] = _mm(h, wv_ref[...], preferred_element_type=F32).astype(BF16)


def _memkv(mem, g, wk, wv):
    B, M, D = mem.shape
    blk = pl.BlockSpec((1, M, D), lambda b: (b, 0, 0))
    return pl.pallas_call(
        _memkv_kernel,
        grid=(B,),
        in_specs=[blk, _const_spec((1, D)), _const_spec(wk.shape), _const_spec(wv.shape)],
        out_specs=(blk, blk),
        out_shape=(jax.ShapeDtypeStruct(mem.shape, BF16), jax.ShapeDtypeStruct(mem.shape, BF16)),
        compiler_params=_cparams("parallel"),
    )(mem, g.reshape(1, D), wk, wv)


def _xattn_kernel(x_ref, pre_ref, wq_ref, k_ref, v_ref, wo_ref, post_ref, o_ref):
    x = x_ref[0]
    h = _rms(x, pre_ref[...]).astype(BF16)
    q = (_mm(h, wq_ref[...], preferred_element_type=F32) * (MEM_HEAD_DIM ** -0.5)).astype(BF16)
    heads = []
    for hd in range(MEM_HEADS):
        cols = slice(hd * MEM_HEAD_DIM, (hd + 1) * MEM_HEAD_DIM)
        s = _mmg(q[:, cols], k_ref[0, :, cols], _NT, preferred_element_type=F32)
        p = jnp.exp(s - jnp.max(s, axis=-1, keepdims=True))
        den = jnp.sum(p, axis=-1, keepdims=True)
        oh = _mm(p.astype(BF16), v_ref[0, :, cols], preferred_element_type=F32) / den
        heads.append(oh.astype(BF16))
    o = jnp.concatenate(heads, axis=1)
    y = _mm(o, wo_ref[...], preferred_element_type=F32)
    o_ref[0] = x + _rms(y, post_ref[...])


def _xattn(x, pre_g, wq, km, vm, wo, post_g):
    B, T, D = x.shape
    M = km.shape[1]
    tm = min(TM, T)
    row = pl.BlockSpec((1, tm, D), lambda b, i: (b, i, 0))
    kv = pl.BlockSpec((1, M, D), lambda b, i: (b, 0, 0))
    return pl.pallas_call(
        _xattn_kernel,
        grid=(B, T // tm),
        in_specs=[row, _const_spec((1, D)), _const_spec(wq.shape), kv, kv, _const_spec(wo.shape),
                  _const_spec((1, D))],
        out_specs=row,
        out_shape=jax.ShapeDtypeStruct(x.shape, F32),
        compiler_params=_cparams("parallel", "parallel"),
    )(x, pre_g.reshape(1, D), wq, km, vm, wo, post_g.reshape(1, D))


def _hybrid_mixer(x, pre_g, w_in, cmp_pe, cmp_w1, cmp_w2, nsa_out_g, ret_gn_g, w_out, post_g):
    q_hm, cmp, ks, vst, kw, vwt, gt, qr, kr, vr, gr = _inproj(x, pre_g, _inproj_weight(w_in))
    c_rm, c_t = _compress(cmp, *_compress_weights(cmp_pe, cmp_w1, cmp_w2))
    oc, sel, flags = _select(q_hm, c_rm, c_t)
    o_nsa = _attn(q_hm, ks, vst, kw, vwt, sel, flags, oc, gt, _attn_gain(nsa_out_g))
    o_ret = _retention(qr, kr, vr, gr, ret_gn_g)
    wn, wr = _outproj_weights(w_out)
    return _outproj(x, o_nsa, o_ret, wn, wr, post_g)


def kernel(x, mem, ffn1_pre_g, ffn1_w_gate, ffn1_w_up, ffn1_w_down, ffn1_post_g, mix_pre_g, w_in, cmp_pe, cmp_w1, cmp_w2, nsa_out_g, ret_gn_g, w_out, mix_post_g, xa_pre_g, xa_mem_g, xa_wq, xa_wk, xa_wv, xa_wo, xa_post_g, ffn2_pre_g, ffn2_w_gate, ffn2_w_up, ffn2_w_down, ffn2_post_g):
    depth = w_in.shape[0]
    bf = lambda w: w.astype(BF16)
    for l in range(depth):
        x = _ffn(x, ffn1_pre_g[l], bf(ffn1_w_gate[l]), bf(ffn1_w_up[l]), bf(ffn1_w_down[l]),
                 ffn1_post_g[l])
        x = _hybrid_mixer(x, mix_pre_g[l], w_in[l], cmp_pe[l], cmp_w1[l], cmp_w2[l], nsa_out_g[l],
                          ret_gn_g[l], w_out[l], mix_post_g[l])
        km, vm = _memkv(mem, xa_mem_g[l], bf(xa_wk[l]), bf(xa_wv[l]))
        x = _xattn(x, xa_pre_g[l], bf(xa_wq[l]), km, vm, bf(xa_wo[l]), xa_post_g[l])
        x = _ffn(x, ffn2_pre_g[l], bf(ffn2_w_gate[l]), bf(ffn2_w_up[l]), bf(ffn2_w_down[l]),
                 ffn2_post_g[l])
    return x
```

```python
import functools

import numpy as np
import jax
import jax.numpy as jnp
from jax import lax
from jax.experimental import pallas as pl
from jax.experimental.pallas import tpu as pltpu

F32 = jnp.float32
BF16 = jnp.bfloat16

D_MODEL = 1024
NSA_HEADS = 8
NSA_HEAD_DIM = 64
NSA_KV_HEADS = 2
NSA_GROUP = NSA_HEADS // NSA_KV_HEADS
NSA_WIDTH = NSA_HEADS * NSA_HEAD_DIM
KV_WIDTH = NSA_KV_HEADS * NSA_HEAD_DIM
CMP_LEN = 32
CMP_STRIDE = 16
CMP_HIDDEN = 256
SLC_LEN = 64
SLC_TOPN = 16
WINDOW = 512
RET_HEADS = 4
RET_HEAD_DIM = 128
RET_WIDTH = RET_HEADS * RET_HEAD_DIM
RET_CHUNK = 128
MEM_HEADS = 4
MEM_HEAD_DIM = D_MODEL // MEM_HEADS
D_FF = 2816
EPS = 1e-6
NEG = -1e30
TAKEN = -3e38
GATE_ROWS = 32

IN_SIZES = (NSA_WIDTH, KV_WIDTH, KV_WIDTH, KV_WIDTH, KV_WIDTH, KV_WIDTH, KV_WIDTH,
            NSA_HEADS * 3, RET_WIDTH, RET_WIDTH, RET_WIDTH, RET_WIDTH)
IN_OFFSETS = tuple(int(o) for o in np.cumsum(IN_SIZES)[:-1])

LANES = 128
QPAD = NSA_HEADS * LANES
LOG2E = 1.4426950408889634
ALIBI2 = tuple(float(2.0 ** (-8.0 * i / NSA_HEADS)) * LOG2E for i in range(1, NSA_HEADS + 1))
ONES_ROWS = 16
VT_ROWS = KV_WIDTH + ONES_ROWS
CH = 256

TM = 512
TQ = 128
TK = 512
WKEYS = WINDOW + TQ
VMEM_LIMIT = 56 * 1024 * 1024

_NT = (((1,), (1,)), ((), ()))
_TN = (((0,), (0,)), ((), ()))


def _mm(a, b, preferred_element_type=F32):
    return jnp.dot(a, b, preferred_element_type=preferred_element_type)


def _mmg(a, b, dims, preferred_element_type=F32):
    return lax.dot_general(a, b, dims, preferred_element_type=preferred_element_type)


def _cparams(*sem):
    return pltpu.CompilerParams(dimension_semantics=sem, vmem_limit_bytes=VMEM_LIMIT)


def _rms(x, g):
    return x * lax.rsqrt(jnp.mean(x * x, axis=-1, keepdims=True) + EPS) * g


def _const_spec(shape):
    nd = len(shape)
    return pl.BlockSpec(shape, lambda *_: (0,) * nd, pipeline_mode=pl.Buffered(1))


def _ffn_kernel(x_ref, pre_ref, wg_ref, wu_ref, wd_ref, post_ref, o_ref):
    x = x_ref[0]
    h = _rms(x, pre_ref[...]).astype(BF16)
    g = _mm(h, wg_ref[...], preferred_element_type=F32)
    u = _mm(h, wu_ref[...], preferred_element_type=F32)
    a = (g * jax.nn.sigmoid(g) * u).astype(BF16)
    y = _mm(a, wd_ref[...], preferred_element_type=F32)
    o_ref[0] = x + 0.5 * _rms(y, post_ref[...])


def _ffn(x, pre_g, wg, wu, wd, post_g):
    B, T, D = x.shape
    tm = min(TM, T)
    return pl.pallas_call(
        _ffn_kernel,
        grid=(B, T // tm),
        in_specs=[pl.BlockSpec((1, tm, D), lambda b, i: (b, i, 0)),
                  _const_spec((1, D)), _const_spec(wg.shape), _const_spec(wu.shape),
                  _const_spec(wd.shape), _const_spec((1, D))],
        out_specs=pl.BlockSpec((1, tm, D), lambda b, i: (b, i, 0)),
        out_shape=jax.ShapeDtypeStruct(x.shape, F32),
        compiler_params=_cparams("parallel", "parallel"),
    )(x, pre_g.reshape(1, D), wg, wu, wd, post_g.reshape(1, D))


_C_Q = 0
_C_CMP = _C_Q + NSA_WIDTH
_C_KS = _C_CMP + 2 * KV_WIDTH
_C_VS = _C_KS + KV_WIDTH
_C_KW = _C_VS + KV_WIDTH
_C_VW = _C_KW + KV_WIDTH
_C_GATE = _C_VW + KV_WIDTH
_C_QR = _C_GATE + LANES
_C_KR = _C_QR + RET_WIDTH
_C_VR = _C_KR + RET_WIDTH
_C_GR = _C_VR + RET_WIDTH
_C_END = _C_GR + RET_WIDTH


def _inproj_kernel(x_ref, pre_ref, w_ref, q_ref, cmp_ref, ks_ref, vst_ref, kw_ref, vwt_ref,
                   gt_ref, qr_ref, kr_ref, vr_ref, gr_ref):
    tm = x_ref.shape[1]
    h = _rms(x_ref[0], pre_ref[...]).astype(BF16)

    def proj(lo, hi):
        return _mm(h, w_ref[:, lo:hi], preferred_element_type=F32)

    q = proj(_C_Q, _C_CMP)
    zero = jnp.zeros((NSA_HEAD_DIM, TQ), F32)
    for j in range(tm // TQ):
        for pair in range(NSA_HEADS // 2):
            t = q[j * TQ:(j + 1) * TQ, pair * LANES:(pair + 1) * LANES].T
            for k in range(2):
                hh = 2 * pair + k
                dims = t[k * NSA_HEAD_DIM:(k + 1) * NSA_HEAD_DIM, :]
                slab = [dims, zero] if hh // NSA_GROUP == 0 else [zero, dims]
                q_ref[0, j, :, hh * TQ:(hh + 1) * TQ] = jnp.concatenate(slab, axis=0).astype(BF16)
    c = proj(_C_CMP, _C_KS)
    cmp_ref[0, 0] = c[:, :KV_WIDTH].astype(BF16)
    cmp_ref[0, 1] = c[:, KV_WIDTH:].astype(BF16)
    ks_ref[0] = proj(_C_KS, _C_VS).astype(BF16)
    kw_ref[0] = proj(_C_KW, _C_VW).astype(BF16)
    vs = proj(_C_VS, _C_KW)
    vw = proj(_C_VW, _C_GATE)
    ones = jnp.ones((ONES_ROWS, LANES), BF16)
    for j in range(tm // LANES):
        vst_ref[0, j, :KV_WIDTH, :] = vs[j * LANES:(j + 1) * LANES, :].T.astype(BF16)
        vwt_ref[0, j, :KV_WIDTH, :] = vw[j * LANES:(j + 1) * LANES, :].T.astype(BF16)
        vst_ref[0, j, KV_WIDTH:, :] = ones
        vwt_ref[0, j, KV_WIDTH:, :] = ones
    gates_t = proj(_C_GATE, _C_QR).T
    for j in range(tm // TQ):
        gt_ref[0, j] = gates_t[:GATE_ROWS, j * TQ:(j + 1) * TQ]
    qr_ref[0] = proj(_C_QR, _C_KR).astype(BF16)
    kr_ref[0] = proj(_C_KR, _C_VR).astype(BF16)
    vr_ref[0] = proj(_C_VR, _C_GR).astype(BF16)
    gr_ref[0] = proj(_C_GR, _C_END)


def _inproj_weight(w_in):
    parts = jnp.split(w_in, IN_OFFSETS, axis=-1)
    q_n, kc, vc, ks, vs, kw, vw, gates, q_r, k_r, v_r, g_r = parts
    scale = NSA_HEAD_DIM ** -0.5 * LOG2E
    gates_p = jnp.pad(gates, ((0, 0), (0, LANES - gates.shape[1])))
    w = jnp.concatenate([q_n * scale, kc, vc, ks, vs, kw, vw, gates_p, q_r, k_r, v_r, g_r], axis=1)
    assert w.shape[1] == _C_END
    return w.astype(BF16)


def _inproj(x, pre_g, w_all):
    B, T, D = x.shape
    tm = min(TM, T)
    nb = tm // LANES
    row = lambda w: pl.BlockSpec((1, tm, w), lambda b, i: (b, i, 0))
    out_shape = (
        jax.ShapeDtypeStruct((B, T // TQ, LANES, NSA_HEADS * TQ), BF16),
        jax.ShapeDtypeStruct((B, 2, T, KV_WIDTH), BF16),
        jax.ShapeDtypeStruct((B, T, KV_WIDTH), BF16),
        jax.ShapeDtypeStruct((B, T // LANES, VT_ROWS, LANES), BF16),
        jax.ShapeDtypeStruct((B, T, KV_WIDTH), BF16),
        jax.ShapeDtypeStruct((B, T // LANES, VT_ROWS, LANES), BF16),
        jax.ShapeDtypeStruct((B, T // TQ, GATE_ROWS, TQ), F32),
        jax.ShapeDtypeStruct((B, T, RET_WIDTH), BF16),
        jax.ShapeDtypeStruct((B, T, RET_WIDTH), BF16),
        jax.ShapeDtypeStruct((B, T, RET_WIDTH), BF16),
        jax.ShapeDtypeStruct((B, T, RET_WIDTH), F32),
    )
    vt_spec = pl.BlockSpec((1, nb, VT_ROWS, LANES), lambda b, i: (b, i, 0, 0))
    out_specs = (
        pl.BlockSpec((1, tm // TQ, LANES, NSA_HEADS * TQ), lambda b, i: (b, i, 0, 0)),
        pl.BlockSpec((1, 2, tm, KV_WIDTH), lambda b, i: (b, 0, i, 0)),
        row(KV_WIDTH), vt_spec, row(KV_WIDTH), vt_spec,
        pl.BlockSpec((1, tm // TQ, GATE_ROWS, TQ), lambda b, i: (b, i, 0, 0)),
        row(RET_WIDTH), row(RET_WIDTH), row(RET_WIDTH), row(RET_WIDTH),
    )
    return pl.pallas_call(
        _inproj_kernel,
        grid=(B, T // tm),
        in_specs=[pl.BlockSpec((1, tm, D), lambda b, i: (b, i, 0)),
                  _const_spec((1, D)), _const_spec(w_all.shape)],
        out_specs=out_specs,
        out_shape=out_shape,
        compiler_params=_cparams("parallel", "parallel"),
    )(x, pre_g.reshape(1, D), w_all)


def _compress_kernel(a_ref, pe_ref, w1_ref, w2_ref, rm_ref, t_ref, hi_scr):
    ncp = a_ref.shape[2]
    a = a_ref[0, 0].astype(F32)
    a_lo = (a + pe_ref[0, 0:1, :]).astype(BF16)
    a_hi = (a + pe_ref[0, 1:2, :]).astype(BF16)
    c = jnp.zeros((ncp, KV_WIDTH), F32)
    hi_scr[pl.ds(ncp, 8), :] = jnp.zeros((8, CMP_HIDDEN), F32)
    for g in range(NSA_KV_HEADS):
        p_lo = _mm(a_lo, w1_ref[0, 2 * g], preferred_element_type=F32)
        hi_scr[pl.ds(0, ncp), :] = _mm(a_hi, w1_ref[0, 2 * g + 1], preferred_element_type=F32)
        hdn = p_lo + hi_scr[pl.ds(1, ncp), :]
        hdn = (hdn * jax.nn.sigmoid(hdn)).astype(BF16)
        c = c + _mm(hdn, w2_ref[0, g], preferred_element_type=F32)
    rows = lax.broadcasted_iota(jnp.int32, c.shape, 0)
    c = jnp.where(rows < ncp - 1, c, 0.0)
    rm_ref[0, 0] = c.astype(BF16)
    for j in range(ncp // CH):
        t_ref[0, 0, j] = c[j * CH:(j + 1) * CH, :].T.astype(BF16)


def _compress_weights(cmp_pe, cmp_w1, cmp_w2):
    half = CMP_LEN // 2
    pe = cmp_pe.reshape(2, 2, half, 1, NSA_HEAD_DIM)
    pe = jnp.broadcast_to(pe, (2, 2, half, NSA_KV_HEADS, NSA_HEAD_DIM)).reshape(2, 2, half * KV_WIDTH)
    w1 = cmp_w1.reshape(2, 2, half, NSA_HEAD_DIM, CMP_HIDDEN)
    w1e = []
    for g in range(NSA_KV_HEADS):
        for hf in range(2):
            z = jnp.zeros((2, half, NSA_KV_HEADS, NSA_HEAD_DIM, CMP_HIDDEN), F32)
            z = z.at[:, :, g].set(w1[:, hf])
            w1e.append(z.reshape(2, half * KV_WIDTH, CMP_HIDDEN))
    w1e = jnp.stack(w1e, axis=1).astype(BF16)
    w2e = []
    for g in range(NSA_KV_HEADS):
        z = jnp.zeros((2, CMP_HIDDEN, NSA_KV_HEADS, NSA_HEAD_DIM), F32)
        z = z.at[:, :, g].set(cmp_w2)
        w2e.append(z.reshape(2, CMP_HIDDEN, KV_WIDTH))
    w2e = jnp.stack(w2e, axis=1).astype(BF16)
    return pe, w1e, w2e


def _compress(cmp, pe, w1e, w2e):
    B, _, T, _ = cmp.shape
    ncp = T // CMP_STRIDE
    a = cmp.reshape(B, 2, ncp, CMP_STRIDE * KV_WIDTH)
    kw = CMP_STRIDE * KV_WIDTH
    return pl.pallas_call(
        _compress_kernel,
        grid=(B, 2),
        in_specs=[pl.BlockSpec((1, 1, ncp, kw), lambda b, k: (b, k, 0, 0)),
                  pl.BlockSpec((1, 2, kw), lambda b, k: (k, 0, 0)),
                  pl.BlockSpec((1, 4, kw, CMP_HIDDEN), lambda b, k: (k, 0, 0, 0)),
                  pl.BlockSpec((1, 2, CMP_HIDDEN, KV_WIDTH), lambda b, k: (k, 0, 0, 0))],
        out_specs=(pl.BlockSpec((1, 1, ncp, KV_WIDTH), lambda b, k: (b, k, 0, 0)),
                   pl.BlockSpec((1, 1, ncp // CH, KV_WIDTH, CH), lambda b, k: (b, k, 0, 0, 0))),
        out_shape=(jax.ShapeDtypeStruct((B, 2, ncp, KV_WIDTH), BF16),
                   jax.ShapeDtypeStruct((B, 2, ncp // CH, KV_WIDTH, CH), BF16)),
        scratch_shapes=[pltpu.VMEM((ncp + 8, CMP_HIDDEN), F32)],
        compiler_params=_cparams("parallel", "parallel"),
    )(a, pe, w1e, w2e)


def _select_kernel(q_ref, kc_ref, vct_ref, ovl_ref, tab_ref, grp_ref, oc_ref, sel_ref, flag_ref,
                   s_scr, p_scr, imp_scr, oc_scr):
    tq = TQ
    nsel = ovl_ref.shape[1]
    q0 = pl.program_id(1) * tq
    qt = q_ref[0, 0]
    span = CH * CMP_STRIDE
    nch = (q0 + tq - CMP_LEN) // span + 1
    nfull = jnp.maximum((q0 - (span + CMP_LEN - CMP_STRIDE - 1)) // span + 1, 0)
    mrel = (lax.broadcasted_iota(jnp.int32, (CH, tq), 0) * CMP_STRIDE + (CMP_LEN - 1)
            - lax.broadcasted_iota(jnp.int32, (CH, tq), 1))
    heads = [slice(hh * tq, (hh + 1) * tq) for hh in range(NSA_HEADS)]

    def chunk_rows(c):
        return pl.ds(pl.multiple_of(c * CH, CH), CH)

    def scores(masked):
        def body(c, ms):
            rows = chunk_rows(c)
            s_c = _mm(kc_ref[0, 0, rows, :], qt, preferred_element_type=F32)
            ok = mrel <= q0 - c * span
            out = []
            for hh, sl in enumerate(heads):
                s = s_c[:, sl] + tab_ref[hh, rows, :]
                if masked:
                    s = jnp.where(ok, s, NEG)
                s_scr[rows, sl] = s
                out.append(jnp.maximum(ms[hh], jnp.max(s, axis=0, keepdims=True)))
            return tuple(out)
        return body

    m0 = tuple(jnp.full((1, tq), NEG, F32) for _ in heads)
    ms = lax.fori_loop(0, nfull, scores(False), m0)
    ms = lax.fori_loop(nfull, nch, scores(True), ms)

    def probs(masked):
        def body(c, dens):
            rows = chunk_rows(c)
            ok = mrel <= q0 - c * span
            out = []
            for hh, sl in enumerate(heads):
                p = jnp.exp2(s_scr[rows, sl] - ms[hh])
                if masked:
                    p = jnp.where(ok, p, 0.0)
                s_scr[rows, sl] = p
                out.append(dens[hh] + jnp.sum(p, axis=0, keepdims=True))
            return tuple(out)
        return body

    d0 = tuple(jnp.zeros((1, tq), F32) for _ in heads)
    dens = lax.fori_loop(0, nfull, probs(False), d0)
    dens = lax.fori_loop(nfull, nch, probs(True), dens)
    rden = [1.0 / jnp.maximum(d, 1e-30) for d in dens]

    imp_scr[...] = jnp.zeros_like(imp_scr)
    oc_scr[...] = jnp.zeros_like(oc_scr)

    def outputs(c, carry):
        rows = chunk_rows(c)
        for g in range(NSA_KV_HEADS):
            psum = jnp.zeros((CH, tq), F32)
            for r in range(NSA_GROUP):
                hh = g * NSA_GROUP + r
                pn = s_scr[rows, heads[hh]] * rden[hh]
                psum = psum + pn
                p_scr[:, heads[hh]] = pn.astype(BF16)
            p_hi = psum.astype(BF16)
            p_lo = (psum - p_hi.astype(F32)).astype(BF16)
            imp_scr[g] += (_mm(ovl_ref[c], p_hi, preferred_element_type=F32)
                           + _mm(ovl_ref[c], p_lo, preferred_element_type=F32))
        oc_scr[...] += _mm(vct_ref[0, 0, c], p_scr[...], preferred_element_type=F32)
        return carry

    lax.fori_loop(0, nch, outputs, 0)
    for hh, sl in enumerate(heads):
        oc_ref[0, 0, hh] = oc_scr[pl.ds((hh // NSA_GROUP) * NSA_HEAD_DIM, NSA_HEAD_DIM), sl]

    n_top = min(SLC_TOPN, nsel)
    n_seen = (q0 + tq - 1) // SLC_LEN + 1
    sizes = sorted({r for r in (nsel * i // 8 for i in range(1, 9)) if r >= n_top and r % 8 == 0})

    def select(rows):
        jb = lax.broadcasted_iota(jnp.int32, (rows, tq), 0)
        cur = (q0 + lax.broadcasted_iota(jnp.int32, (rows, tq), 1)) // SLC_LEN
        forced = (jb == 0) | (jb == cur) | (jb == cur - 1)
        cur1 = (q0 + lax.broadcasted_iota(jnp.int32, (1, tq), 1)) // SLC_LEN
        quota = n_top - (1 + (cur1 >= 1).astype(jnp.int32) + (cur1 >= 2).astype(jnp.int32))
        imps = tuple(jnp.where(forced, TAKEN, jnp.where(jb <= cur, imp_scr[g, pl.ds(0, rows), :], -1.0))
                     for g in range(NSA_KV_HEADS))

        def pick(i, imps):
            out = []
            for imp in imps:
                best = jnp.max(imp, axis=0, keepdims=True)
                first = jnp.min(jnp.where(imp == best, jb, rows), axis=0, keepdims=True)
                first = jnp.where(i < quota, first, -1)
                out.append(jnp.where(jb == first, TAKEN, imp))
            return tuple(out)

        rounds = n_top - jnp.where(q0 >= 2 * SLC_LEN, 3, 1)
        imps = lax.fori_loop(0, rounds, pick, imps)
        ones = jnp.ones((ONES_ROWS, tq), BF16)
        cnt = jnp.zeros((ONES_ROWS, rows), F32)
        for g in range(NSA_KV_HEADS):
            taken = imps[g] < -2.0
            sel_ref[0, 0, g,pl.ds(0, rows), :] = jnp.where(taken, 0.0, NEG)
            if rows < nsel:
                sel_ref[0, 0, g,pl.ds(rows, nsel - rows), :] = jnp.full((nsel - rows, tq), NEG, F32)
            cnt = cnt + _mmg(ones, jnp.where(taken, 1.0, 0.0).astype(BF16), _NT,
                             preferred_element_type=F32)
        act = _mm(jnp.minimum(cnt, 1.0).astype(BF16), grp_ref[pl.ds(0, rows), :],
                  preferred_element_type=F32)
        flag_ref[0, 0] = (act[0:1, :] > 0.0).astype(jnp.int32)

    lo = 0
    for rows in sizes:
        cond = n_seen > lo
        if rows < nsel:
            cond = cond & (n_seen <= rows)
        pl.when(cond)(functools.partial(select, rows))
        lo = rows


def _overlap_t(T):
    ncp = T // CMP_STRIDE
    nc = (T - CMP_LEN) // CMP_STRIDE + 1
    nsel = T // SLC_LEN
    c_start = np.arange(nc) * CMP_STRIDE
    c_end = c_start + CMP_LEN - 1
    s_start = np.arange(nsel) * SLC_LEN
    s_end = s_start + SLC_LEN - 1
    ov = np.clip(np.minimum(c_end[:, None], s_end[None, :])
                 - np.maximum(c_start[:, None], s_start[None, :]) + 1, 0, None) / CMP_LEN
    out = np.zeros((nsel, ncp), np.float32)
    out[:, :nc] = ov.T
    out = out.reshape(nsel, ncp // CH, CH).transpose(1, 0, 2)
    return jnp.asarray(out, dtype=BF16)


def _select_tables(T):
    ncp = T // CMP_STRIDE
    nsel = T // SLC_LEN
    nkt = T // min(TK, T)
    n = np.arange(ncp, dtype=np.float64)[:, None]
    l = np.arange(TQ, dtype=np.float64)[None, :]
    dist = l - (n * CMP_STRIDE + (CMP_LEN - 1) / 2.0)
    tab = np.stack([-a * dist for a in ALIBI2]).astype(np.float32)
    grp = (np.arange(nsel)[:, None] // (nsel // nkt) == np.arange(nkt)[None, :]).astype(np.float32)
    grp[0, :] = 0.0
    return jnp.asarray(tab), jnp.asarray(grp, dtype=BF16)


def _select(q_hm, c_rm, c_t):
    B, T = q_hm.shape[0], q_hm.shape[1] * TQ
    ncp = T // CMP_STRIDE
    nsel = T // SLC_LEN
    nqt = T // TQ
    nkt = T // min(TK, T)
    tab, grp = _select_tables(T)
    return pl.pallas_call(
        _select_kernel,
        grid=(B, nqt),
        in_specs=[pl.BlockSpec((1, 1, LANES, NSA_HEADS * TQ), lambda b, i: (b, i, 0, 0)),
                  pl.BlockSpec((1, 1, ncp, KV_WIDTH), lambda b, i: (b, 0, 0, 0)),
                  pl.BlockSpec((1, 1, ncp // CH, KV_WIDTH, CH), lambda b, i: (b, 1, 0, 0, 0)),
                  _const_spec((ncp // CH, nsel, CH)), _const_spec(tab.shape), _const_spec(grp.shape)],
        out_specs=(pl.BlockSpec((1, 1, NSA_HEADS, NSA_HEAD_DIM, TQ), lambda b, i: (b, i, 0, 0, 0)),
                   pl.BlockSpec((1, 1, NSA_KV_HEADS, nsel, TQ), lambda b, i: (b, i, 0, 0, 0)),
                   pl.BlockSpec((1, 1, 1, nkt), lambda b, i: (b, i, 0, 0))),
        out_shape=(jax.ShapeDtypeStruct((B, nqt, NSA_HEADS, NSA_HEAD_DIM, TQ), F32),
                   jax.ShapeDtypeStruct((B, nqt, NSA_KV_HEADS, nsel, TQ), F32),
                   jax.ShapeDtypeStruct((B, nqt, 1, nkt), jnp.int32)),
        scratch_shapes=[pltpu.VMEM((ncp, NSA_HEADS * TQ), F32),
                        pltpu.VMEM((CH, NSA_HEADS * TQ), BF16),
                        pltpu.VMEM((NSA_KV_HEADS, nsel, TQ), F32),
                        pltpu.VMEM((KV_WIDTH, NSA_HEADS * TQ), F32)],
        compiler_params=_cparams("parallel", "arbitrary"),
    )(q_hm, c_rm, c_t, _overlap_t(T), tab, grp)


def _attn_kernel(flags_ref, q_ref, ks_ref, vst_ref, kw_ref, vwt_ref, sel_ref, oc_ref, gt_ref,
                 gain_ref, kf_ref, kh_ref, qf_ref, wb_ref, o_ref, acc_scr, p_scr, m_scr, alpha_scr,
                 pw_scr, ow_scr, qf_scr):
    tq = TQ
    T = ks_ref.shape[1]
    qi = pl.program_id(1)
    q0 = qi * tq
    tk = kf_ref.shape[0]
    wk = wb_ref.shape[1]
    nkt = T // tk
    bpt = tk // SLC_LEN
    vpt = tk // LANES
    assert bpt == 8
    qt = q_ref[0, 0]
    heads = [slice(hh * tq, (hh + 1) * tq) for hh in range(NSA_HEADS)]

    qf_scr[pl.ds(0, LANES), :] = qt
    qf_scr[pl.ds(LANES, LANES), :] = qf_ref[...]
    row8 = lax.broadcasted_iota(jnp.int32, (8, tq), 0)

    def digits3(v):
        d1 = v.astype(BF16).astype(F32)
        d2 = (v - d1).astype(BF16).astype(F32)
        return d1, d2, v - d1 - d2

    def tile(kt, rows, first, head=False):
        k0 = kt * tk
        shift = (q0 - k0).astype(F32)
        for g in range(NSA_KV_HEADS):
            sb = sel_ref[0, 0, g,pl.ds(pl.multiple_of(kt * bpt, bpt), bpt), :]
            for r in range(NSA_GROUP):
                hh = g * NSA_GROUP + r
                d = digits3(jnp.full((8, tq), -ALIBI2[hh], F32) * shift)
                low = jnp.where(row8 == 0, d[0], jnp.where(row8 == 1, d[1], jnp.where(row8 == 2, d[2], 0.0)))
                if head:
                    e = digits3(jnp.full((8, tq), -ALIBI2[hh], F32) * q0.astype(F32))
                    low = jnp.where(row8 == 3, e[0], jnp.where(row8 == 4, e[1], jnp.where(row8 == 5, e[2], low)))
                qf_scr[pl.ds(LANES, 16), heads[hh]] = jnp.concatenate([sb, low], axis=0).astype(BF16)
                if head:
                    qf_scr[pl.ds(LANES + 16, 16), heads[hh]] = jnp.concatenate(
                        [sel_ref[0, 0, g,pl.ds(0, 8), :], jnp.zeros((8, tq), F32)], axis=0).astype(BF16)
        lhs = jnp.concatenate([ks_ref[0, pl.ds(pl.multiple_of(k0, tk), rows), :],
                               kf_ref[pl.ds(0, rows), :]], axis=1)
        vts = [vst_ref[0, kt * vpt + j] for j in range(rows // LANES)]
        if head:
            lhs = jnp.concatenate(
                [jnp.concatenate([ks_ref[0, pl.ds(0, LANES), :], kh_ref[...]], axis=1), lhs], axis=0)
            vts = [vst_ref[0, 0]] + vts
            rows = rows + LANES
        vt = jnp.concatenate(vts, axis=1)
        if first:
            c_i = lax.broadcasted_iota(jnp.int32, (rows, tq), 0)
            l_i = lax.broadcasted_iota(jnp.int32, (rows, tq), 1)
            if head:
                causal = (c_i < LANES) | (c_i + (k0 - q0 - LANES) <= l_i)
            else:
                causal = c_i + (k0 - q0) <= l_i
        s_all = _mm(lhs, qf_scr[...], preferred_element_type=F32)
        for hh, sl in enumerate(heads):
            s = s_all[:, sl]
            if first:
                s = jnp.where(causal, s, NEG)
                m_new = jnp.max(s, axis=0, keepdims=True)
            else:
                m_old = m_scr[:, sl]
                m_new = jnp.maximum(m_old, jnp.max(s, axis=0, keepdims=True))
                alpha_scr[:, sl] = jnp.exp2(m_old - m_new)
            m_scr[:, sl] = m_new
            p_scr[pl.ds(0, rows), sl] = jnp.exp2(s - m_new).astype(BF16)
        pv = _mm(vt, p_scr[pl.ds(0, rows), :], preferred_element_type=F32)
        if first:
            acc_scr[...] = pv
        else:
            acc_scr[...] = acc_scr[...] * alpha_scr[...] + pv

    kt_diag = q0 // tk
    fbase = (pl.program_id(0) * pl.num_programs(1) + qi) * nkt
    with_head = (kt_diag > 0) & (flags_ref[fbase] == 0)
    for jj in range(tk // tq):
        here = qi % (tk // tq) == jj
        pl.when(here & with_head)(functools.partial(tile, kt_diag, (jj + 1) * tq, True, True))
        pl.when(here & jnp.logical_not(with_head))(
            functools.partial(tile, kt_diag, (jj + 1) * tq, True, False))

    def body(kt, carry):
        @pl.when(flags_ref[fbase + kt] != 0)
        def _():
            tile(kt, tk, False)
        return carry

    lax.fori_loop(0, kt_diag, body, 0)

    def window(ws, general):
        sw_all = _mm(kw_ref[0, pl.ds(pl.multiple_of(ws, LANES), wk), :], qt,
                     preferred_element_type=F32)
        vwt = jnp.concatenate([vwt_ref[0, ws // LANES + j] for j in range(wk // LANES)], axis=1)
        if general:
            kpos = ws + lax.broadcasted_iota(jnp.int32, (wk, tq), 0)
            dist = q0 + lax.broadcasted_iota(jnp.int32, (wk, tq), 1) - kpos
            wmask = (dist >= 0) & (dist < WINDOW)
            distf = dist.astype(F32)
        for hh, sl in enumerate(heads):
            if general:
                s = jnp.where(wmask, sw_all[:, sl] - ALIBI2[hh] * distf, NEG)
            else:
                s = sw_all[:, sl] + wb_ref[hh]
            pw_scr[:, sl] = jnp.exp2(s - jnp.max(s, axis=0, keepdims=True)).astype(BF16)
        ow_scr[...] = _mm(vwt, pw_scr[...], preferred_element_type=F32)

    @pl.when(q0 >= WINDOW)
    def _():
        window(q0 - WINDOW, False)

    @pl.when(q0 < WINDOW)
    def _():
        window(q0 * 0, True)

    gs = jax.nn.sigmoid(gt_ref[0, 0])
    outs = []
    ssq = jnp.zeros((1, tq), F32)
    for hh in range(NSA_HEADS):
        sl = heads[hh]
        ch = pl.ds((hh // NSA_GROUP) * NSA_HEAD_DIM, NSA_HEAD_DIM)
        o_s = acc_scr[ch, sl] / jnp.maximum(acc_scr[KV_WIDTH:KV_WIDTH + 1, sl], 1e-30)
        o_w = ow_scr[ch, sl] / jnp.maximum(ow_scr[KV_WIDTH:KV_WIDTH + 1, sl], 1e-30)
        o = (gs[3 * hh:3 * hh + 1, :] * oc_ref[0, 0, hh] + gs[3 * hh + 1:3 * hh + 2, :] * o_s
             + gs[3 * hh + 2:3 * hh + 3, :] * o_w)
        ssq = ssq + jnp.sum(o * o, axis=0, keepdims=True)
        outs.append(o)
    rinv = lax.rsqrt(ssq / NSA_WIDTH + EPS)
    for j in range(NSA_HEADS // 2):
        pair = jnp.concatenate([outs[2 * j] * rinv * gain_ref[2 * j],
                                outs[2 * j + 1] * rinv * gain_ref[2 * j + 1]], axis=0)
        o_ref[0, :, j * LANES:(j + 1) * LANES] = pair.T.astype(BF16)


def _attn_gain(nsa_out_g):
    g = nsa_out_g.reshape(NSA_HEADS, NSA_HEAD_DIM)
    return jnp.broadcast_to(g[:, :, None], (NSA_HEADS, NSA_HEAD_DIM, TQ))


def _bf16_digits(x, n):
    out = []
    for _ in range(n):
        d = float(np.asarray(x, np.float32).astype(jnp.bfloat16).astype(np.float32))
        out.append(d)
        x = x - d
    return out


def _attn_tables(T):
    tk = min(TK, T)
    wk = min(WKEYS, T)
    c = np.arange(tk)
    kf = np.zeros((tk, LANES), np.float32)
    kf[c, c // SLC_LEN] = 1.0
    kf[:, 8:11] = 1.0
    kf[:, 32:35] = (c // 32)[:, None]
    kf[:, 35:38] = (c % 32)[:, None]
    ch = np.arange(LANES)
    kh = np.zeros((LANES, LANES), np.float32)
    kh[ch, 16 + ch // SLC_LEN] = 1.0
    kh[:, 11:14] = 1.0
    kh[:, 32:35] = (ch // 32)[:, None]
    kh[:, 35:38] = (ch % 32)[:, None]
    qf = np.zeros((LANES, NSA_HEADS * TQ), np.float32)
    for hh, a in enumerate(ALIBI2):
        d = _bf16_digits(a, 3)
        for i in range(3):
            qf[32 + i, hh * TQ:(hh + 1) * TQ] = 32.0 * d[i]
            qf[35 + i, hh * TQ:(hh + 1) * TQ] = d[i]
    l = np.arange(TQ, dtype=np.float64)[None, :]
    dist = WINDOW + l - np.arange(wk, dtype=np.float64)[:, None]
    inside = (dist >= 0) & (dist < WINDOW)
    wb = np.stack([np.where(inside, -a * dist, NEG) for a in ALIBI2]).astype(np.float32)
    return (jnp.asarray(kf, dtype=BF16), jnp.asarray(kh, dtype=BF16), jnp.asarray(qf, dtype=BF16),
            jnp.asarray(wb))


def _attn(q_hm, ks, vst, kw, vwt, sel, flags, oc, gt, gain_b):
    B, T = q_hm.shape[0], q_hm.shape[1] * TQ
    nsel = T // SLC_LEN
    nqt = T // TQ
    kf, kh, qf, wb = _attn_tables(T)
    tk, wk = kf.shape[0], wb.shape[1]
    once = dict(pipeline_mode=pl.Buffered(1))
    full_k = pl.BlockSpec((1, T, KV_WIDTH), lambda b, i, f: (b, 0, 0), **once)
    full_vt = pl.BlockSpec((1, T // LANES, VT_ROWS, LANES), lambda b, i, f: (b, 0, 0, 0), **once)
    const = lambda shape: pl.BlockSpec(shape, lambda b, i, f: (0,) * len(shape), **once)
    grid_spec = pltpu.PrefetchScalarGridSpec(
        num_scalar_prefetch=1,
        grid=(B, nqt),
        in_specs=[pl.BlockSpec((1, 1, LANES, NSA_HEADS * TQ), lambda b, i, f: (b, i, 0, 0)),
                  full_k, full_vt, full_k, full_vt,
                  pl.BlockSpec((1, 1, NSA_KV_HEADS, nsel, TQ), lambda b, i, f: (b, i, 0, 0, 0)),
                  pl.BlockSpec((1, 1, NSA_HEADS, NSA_HEAD_DIM, TQ), lambda b, i, f: (b, i, 0, 0, 0)),
                  pl.BlockSpec((1, 1, GATE_ROWS, TQ), lambda b, i, f: (b, i, 0, 0)),
                  const((NSA_HEADS, NSA_HEAD_DIM, TQ)), const(kf.shape), const(kh.shape),
                  const(qf.shape), const(wb.shape)],
        out_specs=pl.BlockSpec((1, TQ, NSA_WIDTH), lambda b, i, f: (b, i, 0)),
        scratch_shapes=[pltpu.VMEM((VT_ROWS, NSA_HEADS * TQ), F32),
                        pltpu.VMEM((tk + LANES, NSA_HEADS * TQ), BF16),
                        pltpu.VMEM((1, NSA_HEADS * TQ), F32),
                        pltpu.VMEM((1, NSA_HEADS * TQ), F32),
                        pltpu.VMEM((wk, NSA_HEADS * TQ), BF16),
                        pltpu.VMEM((VT_ROWS, NSA_HEADS * TQ), F32),
                        pltpu.VMEM((2 * LANES, NSA_HEADS * TQ), BF16)])
    return pl.pallas_call(
        _attn_kernel,
        grid_spec=grid_spec,
        out_shape=jax.ShapeDtypeStruct((B, T, NSA_WIDTH), BF16),
        compiler_params=_cparams("parallel", "arbitrary"),
    )(flags.reshape(-1), q_hm, ks, vst, kw, vwt, sel, oc, gt, gain_b, kf, kh, qf, wb)


def _retention_kernel(q_ref, k_ref, v_ref, g_ref, dm_ref, qd_ref, kd_ref, gn_ref, o_ref, st_scr,
                      *, chunk_dec):
    rt = q_ref.shape[1]
    C, d = RET_CHUNK, RET_HEAD_DIM

    @pl.when(pl.program_id(1) == 0)
    def _():
        st_scr[...] = jnp.zeros_like(st_scr)

    for c in range(rt // C):
        rows = slice(c * C, (c + 1) * C)
        for h in range(RET_HEADS):
            cols = slice(h * d, (h + 1) * d)
            q = q_ref[0, rows, cols]
            k = k_ref[0, rows, cols]
            v = v_ref[0, rows, cols]
            state = st_scr[h]
            s = _mmg(q, k, _NT, preferred_element_type=F32) * dm_ref[h]
            inner = _mm(s.astype(BF16), v, preferred_element_type=F32)
            cross = _mm(q, state.astype(BF16), preferred_element_type=F32) * qd_ref[h]
            kd = (k.astype(F32) * kd_ref[h]).astype(BF16)
            st_scr[h] = state * chunk_dec[h] + _mmg(kd, v, _TN, preferred_element_type=F32)
            o = inner + cross
            mu = jnp.mean(o, axis=-1, keepdims=True)
            oc = o - mu
            var = jnp.mean(oc * oc, axis=-1, keepdims=True)
            o = oc * lax.rsqrt(var + EPS) * gn_ref[:, cols]
            gg = g_ref[0, rows, cols]
            o_ref[0, rows, cols] = (o * (gg * jax.nn.sigmoid(gg))).astype(BF16)


def _retention_consts():
    C, d = RET_CHUNK, RET_HEAD_DIM
    log_g = np.log(1.0 - np.exp2(-5.0 - np.arange(RET_HEADS, dtype=np.float64)))
    pos = np.arange(C, dtype=np.float64)
    diff = pos[:, None] - pos[None, :]
    scale = d ** -0.5
    dmask = np.where(diff >= 0, np.exp(np.maximum(diff, 0.0) * log_g[:, None, None]), 0.0) * scale
    q_dec = np.exp((pos + 1.0) * log_g[:, None])
    k_dec = np.exp((C - 1.0 - pos) * log_g[:, None]) * scale
    chunk_dec = tuple(float(x) for x in np.exp(C * log_g))
    bc = lambda a: jnp.asarray(np.broadcast_to(a[:, :, None], (RET_HEADS, C, d)).astype(np.float32))
    return jnp.asarray(dmask.astype(np.float32)), bc(q_dec), bc(k_dec), chunk_dec


def _retention(qr, kr, vr, gr, gn_gain):
    B, T, W = qr.shape
    rt = min(2 * TM, T)
    dmask, qd, kd, chunk_dec = _retention_consts()
    row = pl.BlockSpec((1, rt, W), lambda b, i: (b, i, 0))
    cst = _const_spec((RET_HEADS, RET_CHUNK, RET_HEAD_DIM))
    return pl.pallas_call(
        functools.partial(_retention_kernel, chunk_dec=chunk_dec),
        grid=(B, T // rt),
        in_specs=[row, row, row, row, cst, cst, cst, _const_spec((1, W))],
        out_specs=row,
        out_shape=jax.ShapeDtypeStruct((B, T, W), BF16),
        scratch_shapes=[pltpu.VMEM((RET_HEADS, RET_HEAD_DIM, RET_HEAD_DIM), F32)],
        compiler_params=_cparams("parallel", "arbitrary"),
    )(qr, kr, vr, gr, dmask, qd, kd, gn_gain.reshape(1, W))


def _outproj_kernel(x_ref, on_ref, or_ref, wn_ref, wr_ref, post_ref, o_ref):
    y = (_mm(on_ref[0], wn_ref[...], preferred_element_type=F32)
         + _mm(or_ref[0], wr_ref[...], preferred_element_type=F32))
    o_ref[0] = x_ref[0] + _rms(y, post_ref[...])


def _outproj_weights(w_out):
    return w_out[:NSA_WIDTH].astype(BF16), w_out[NSA_WIDTH:].astype(BF16)


def _outproj(x, o_nsa, o_ret, wn, wr, post_g):
    B, T, D = x.shape
    tm = min(2 * TM, T)
    row = lambda w: pl.BlockSpec((1, tm, w), lambda b, i: (b, i, 0))
    return pl.pallas_call(
        _outproj_kernel,
        grid=(B, T // tm),
        in_specs=[row(D), row(NSA_WIDTH), row(RET_WIDTH), _const_spec(wn.shape), _const_spec(wr.shape),
                  _const_spec((1, D))],
        out_specs=row(D),
        out_shape=jax.ShapeDtypeStruct(x.shape, F32),
        compiler_params=_cparams("parallel", "parallel"),
    )(x, o_nsa, o_ret, wn, wr, post_g.reshape(1, D))


def _memkv_kernel(m_ref, g_ref, wk_ref, wv_ref, k_ref, v_ref):
    h = _rms(m_ref[0], g_ref[...]).astype(BF16)
    k_ref[0] = _mm(h, wk_ref[...], preferred_element_type=F32).astype(BF16)
    v_ref[0] = _mm(h, wv_ref[...], preferred_element_type=F32).astype(BF16)


def _memkv(mem, g, wk, wv):
    B, M, D = mem.shape
    blk = pl.BlockSpec((1, M, D), lambda b: (b, 0, 0))
    return pl.pallas_call(
        _memkv_kernel,
        grid=(B,),
        in_specs=[blk, _const_spec((1, D)), _const_spec(wk.shape), _const_spec(wv.shape)],
        out_specs=(blk, blk),
        out_shape=(jax.ShapeDtypeStruct(mem.shape, BF16), jax.ShapeDtypeStruct(mem.shape, BF16)),
        compiler_params=_cparams("parallel"),
    )(mem, g.reshape(1, D), wk, wv)


def _xattn_kernel(x_ref, pre_ref, wq_ref, k_ref, v_ref, wo_ref, post_ref, o_ref):
    x = x_ref[0]
    h = _rms(x, pre_ref[...]).astype(BF16)
    q = (_mm(h, wq_ref[...], preferred_element_type=F32) * (MEM_HEAD_DIM ** -0.5)).astype(BF16)
    heads = []
    for hd in range(MEM_HEADS):
        cols = slice(hd * MEM_HEAD_DIM, (hd + 1) * MEM_HEAD_DIM)
        s = _mmg(q[:, cols], k_ref[0, :, cols], _NT, preferred_element_type=F32)
        p = jnp.exp(s - jnp.max(s, axis=-1, keepdims=True))
        den = jnp.sum(p, axis=-1, keepdims=True)
        oh = _mm(p.astype(BF16), v_ref[0, :, cols], preferred_element_type=F32) / den
        heads.append(oh.astype(BF16))
    o = jnp.concatenate(heads, axis=1)
    y = _mm(o, wo_ref[...], preferred_element_type=F32)
    o_ref[0] = x + _rms(y, post_ref[...])


def _xattn(x, pre_g, wq, km, vm, wo, post_g):
    B, T, D = x.shape
    M = km.shape[1]
    tm = min(2 * TM, T)
    row = pl.BlockSpec((1, tm, D), lambda b, i: (b, i, 0))
    kv = pl.BlockSpec((1, M, D), lambda b, i: (b, 0, 0))
    return pl.pallas_call(
        _xattn_kernel,
        grid=(B, T // tm),
        in_specs=[row, _const_spec((1, D)), _const_spec(wq.shape), kv, kv, _const_spec(wo.shape),
                  _const_spec((1, D))],
        out_specs=row,
        out_shape=jax.ShapeDtypeStruct(x.shape, F32),
        compiler_params=_cparams("parallel", "parallel"),
    )(x, pre_g.reshape(1, D), wq, km, vm, wo, post_g.reshape(1, D))


def _hybrid_mixer(x, pre_g, w_in, cmp_pe, cmp_w1, cmp_w2, nsa_out_g, ret_gn_g, w_out, post_g):
    q_hm, cmp, ks, vst, kw, vwt, gt, qr, kr, vr, gr = _inproj(x, pre_g, _inproj_weight(w_in))
    c_rm, c_t = _compress(cmp, *_compress_weights(cmp_pe, cmp_w1, cmp_w2))
    oc, sel, flags = _select(q_hm, c_rm, c_t)
    o_nsa = _attn(q_hm, ks, vst, kw, vwt, sel, flags, oc, gt, _attn_gain(nsa_out_g))
    o_ret = _retention(qr, kr, vr, gr, ret_gn_g)
    wn, wr = _outproj_weights(w_out)
    return _outproj(x, o_nsa, o_ret, wn, wr, post_g)


def kernel(x, mem, ffn1_pre_g, ffn1_w_gate, ffn1_w_up, ffn1_w_down, ffn1_post_g, mix_pre_g, w_in, cmp_pe, cmp_w1, cmp_w2, nsa_out_g, ret_gn_g, w_out, mix_post_g, xa_pre_g, xa_mem_g, xa_wq, xa_wk, xa_wv, xa_wo, xa_post_g, ffn2_pre_g, ffn2_w_gate, ffn2_w_up, ffn2_w_down, ffn2_post_g):
    depth = w_in.shape[0]
    bf = lambda w: w.astype(BF16)
    for l in range(depth):
        x = _ffn(x, ffn1_pre_g[l], bf(ffn1_w_gate[l]), bf(ffn1_w_up[l]), bf(ffn1_w_down[l]),
                 ffn1_post_g[l])
        x = _hybrid_mixer(x, mix_pre_g[l], w_in[l], cmp_pe[l], cmp_w1[l], cmp_w2[l], nsa_out_g[l],
                          ret_gn_g[l], w_out[l], mix_post_g[l])
        km, vm = _memkv(mem, xa_mem_g[l], bf(xa_wk[l]), bf(xa_wv[l]))
        x = _xattn(x, xa_pre_g[l], bf(xa_wq[l]), km, vm, bf(xa_wo[l]), xa_post_g[l])
        x = _ffn(x, ffn2_pre_g[l], bf(ffn2_w_gate[l]), bf(ffn2_w_up[l]), bf(ffn2_w_down[l]),
                 ffn2_post_g[l])
    return x
```

```python
import functools

import numpy as np
import jax
import jax.numpy as jnp
from jax import lax
from jax.experimental import pallas as pl
from jax.experimental.pallas import tpu as pltpu

F32 = jnp.float32
BF16 = jnp.bfloat16

D_MODEL = 1024
NSA_HEADS = 8
NSA_HEAD_DIM = 64
NSA_KV_HEADS = 2
NSA_GROUP = NSA_HEADS // NSA_KV_HEADS
NSA_WIDTH = NSA_HEADS * NSA_HEAD_DIM
KV_WIDTH = NSA_KV_HEADS * NSA_HEAD_DIM
CMP_LEN = 32
CMP_STRIDE = 16
CMP_HIDDEN = 256
SLC_LEN = 64
SLC_TOPN = 16
WINDOW = 512
RET_HEADS = 4
RET_HEAD_DIM = 128
RET_WIDTH = RET_HEADS * RET_HEAD_DIM
RET_CHUNK = 256
MEM_HEADS = 4
MEM_HEAD_DIM = D_MODEL // MEM_HEADS
D_FF = 2816
EPS = 1e-6
NEG = -1e30
TAKEN = -3e38
GATE_ROWS = 32

IN_SIZES = (NSA_WIDTH, KV_WIDTH, KV_WIDTH, KV_WIDTH, KV_WIDTH, KV_WIDTH, KV_WIDTH,
            NSA_HEADS * 3, RET_WIDTH, RET_WIDTH, RET_WIDTH, RET_WIDTH)
IN_OFFSETS = tuple(int(o) for o in np.cumsum(IN_SIZES)[:-1])

LANES = 128
LOG2E = 1.4426950408889634
ALIBI2 = tuple(float(2.0 ** (-8.0 * i / NSA_HEADS)) * LOG2E for i in range(1, NSA_HEADS + 1))
ONES_ROWS = 16
VT_ROWS = KV_WIDTH + ONES_ROWS
CH = 256

TM = 512
TQ = 128
TK = 512
WKEYS = WINDOW + TQ
VMEM_LIMIT = 56 * 1024 * 1024

_NT = (((1,), (1,)), ((), ()))
_TN = (((0,), (0,)), ((), ()))


def _mm(a, b, preferred_element_type=F32):
    return jnp.dot(a, b, preferred_element_type=preferred_element_type)


def _mmg(a, b, dims, preferred_element_type=F32):
    return lax.dot_general(a, b, dims, preferred_element_type=preferred_element_type)


def _cparams(*sem):
    return pltpu.CompilerParams(dimension_semantics=sem, vmem_limit_bytes=VMEM_LIMIT)


def _rms(x, g):
    return x * lax.rsqrt(jnp.mean(x * x, axis=-1, keepdims=True) + EPS) * g


def _const_spec(shape):
    nd = len(shape)
    return pl.BlockSpec(shape, lambda *_: (0,) * nd, pipeline_mode=pl.Buffered(1))


def _ffn_kernel(x_ref, pre_ref, wg_ref, wu_ref, wd_ref, post_ref, o_ref):
    x = x_ref[0]
    h = _rms(x, pre_ref[...]).astype(BF16)
    g = _mm(h, wg_ref[...], preferred_element_type=F32)
    u = _mm(h, wu_ref[...], preferred_element_type=F32)
    a = (g * jax.nn.sigmoid(g) * u).astype(BF16)
    y = _mm(a, wd_ref[...], preferred_element_type=F32)
    o_ref[0] = x + 0.5 * _rms(y, post_ref[...])


def _ffn(x, pre_g, wg, wu, wd, post_g):
    B, T, D = x.shape
    tm = min(TM, T)
    return pl.pallas_call(
        _ffn_kernel,
        grid=(B, T // tm),
        in_specs=[pl.BlockSpec((1, tm, D), lambda b, i: (b, i, 0)),
                  _const_spec((1, D)), _const_spec(wg.shape), _const_spec(wu.shape),
                  _const_spec(wd.shape), _const_spec((1, D))],
        out_specs=pl.BlockSpec((1, tm, D), lambda b, i: (b, i, 0)),
        out_shape=jax.ShapeDtypeStruct(x.shape, F32),
        compiler_params=_cparams("parallel", "parallel"),
    )(x, pre_g.reshape(1, D), wg, wu, wd, post_g.reshape(1, D))


_C_Q = 0
_C_CMP = _C_Q + NSA_WIDTH
_C_KS = _C_CMP + 2 * KV_WIDTH
_C_VS = _C_KS + KV_WIDTH
_C_KW = _C_VS + KV_WIDTH
_C_VW = _C_KW + KV_WIDTH
_C_GATE = _C_VW + KV_WIDTH
_C_QR = _C_GATE + LANES
_C_KR = _C_QR + RET_WIDTH
_C_VR = _C_KR + RET_WIDTH
_C_GR = _C_VR + RET_WIDTH
_C_END = _C_GR + RET_WIDTH


def _inproj_kernel(x_ref, pre_ref, w_ref, q_ref, cmp_ref, ks_ref, vst_ref, kw_ref, vwt_ref,
                   gt_ref, qr_ref, kr_ref, vr_ref, gr_ref):
    tm = x_ref.shape[1]
    h = _rms(x_ref[0], pre_ref[...]).astype(BF16)

    def proj(lo, hi):
        return _mm(h, w_ref[:, lo:hi], preferred_element_type=F32)

    q = proj(_C_Q, _C_CMP)
    zero = jnp.zeros((NSA_HEAD_DIM, TQ), F32)
    for j in range(tm // TQ):
        for pair in range(NSA_HEADS // 2):
            t = q[j * TQ:(j + 1) * TQ, pair * LANES:(pair + 1) * LANES].T
            for k in range(2):
                hh = 2 * pair + k
                dims = t[k * NSA_HEAD_DIM:(k + 1) * NSA_HEAD_DIM, :]
                slab = [dims, zero] if hh // NSA_GROUP == 0 else [zero, dims]
                q_ref[0, j, :, hh * TQ:(hh + 1) * TQ] = jnp.concatenate(slab, axis=0).astype(BF16)
    c = proj(_C_CMP, _C_KS)
    cmp_ref[0, 0] = c[:, :KV_WIDTH].astype(BF16)
    cmp_ref[0, 1] = c[:, KV_WIDTH:].astype(BF16)
    ks_ref[0] = proj(_C_KS, _C_VS).astype(BF16)
    kw_ref[0] = proj(_C_KW, _C_VW).astype(BF16)
    vs = proj(_C_VS, _C_KW)
    vw = proj(_C_VW, _C_GATE)
    ones = jnp.ones((ONES_ROWS, LANES), BF16)
    for j in range(tm // LANES):
        vst_ref[0, j, :KV_WIDTH, :] = vs[j * LANES:(j + 1) * LANES, :].T.astype(BF16)
        vwt_ref[0, j, :KV_WIDTH, :] = vw[j * LANES:(j + 1) * LANES, :].T.astype(BF16)
        vst_ref[0, j, KV_WIDTH:, :] = ones
        vwt_ref[0, j, KV_WIDTH:, :] = ones
    gates_t = proj(_C_GATE, _C_QR).T
    for j in range(tm // TQ):
        gt_ref[0, j] = gates_t[:GATE_ROWS, j * TQ:(j + 1) * TQ]
    qr_ref[0] = proj(_C_QR, _C_KR).astype(BF16)
    kr_ref[0] = proj(_C_KR, _C_VR).astype(BF16)
    vr_ref[0] = proj(_C_VR, _C_GR).astype(BF16)
    gr_ref[0] = proj(_C_GR, _C_END)


def _inproj_weight(w_in):
    parts = jnp.split(w_in, IN_OFFSETS, axis=-1)
    q_n, kc, vc, ks, vs, kw, vw, gates, q_r, k_r, v_r, g_r = parts
    scale = NSA_HEAD_DIM ** -0.5 * LOG2E
    gates_p = jnp.pad(gates, ((0, 0), (0, LANES - gates.shape[1])))
    w = jnp.concatenate([q_n * scale, kc, vc, ks, vs, kw, vw, gates_p, q_r, k_r, v_r, g_r], axis=1)
    assert w.shape[1] == _C_END
    return w.astype(BF16)


def _inproj(x, pre_g, w_all):
    B, T, D = x.shape
    tm = min(TM, T)
    nb = tm // LANES
    row = lambda w: pl.BlockSpec((1, tm, w), lambda b, i: (b, i, 0))
    out_shape = (
        jax.ShapeDtypeStruct((B, T // TQ, LANES, NSA_HEADS * TQ), BF16),
        jax.ShapeDtypeStruct((B, 2, T, KV_WIDTH), BF16),
        jax.ShapeDtypeStruct((B, T, KV_WIDTH), BF16),
        jax.ShapeDtypeStruct((B, T // LANES, VT_ROWS, LANES), BF16),
        jax.ShapeDtypeStruct((B, T, KV_WIDTH), BF16),
        jax.ShapeDtypeStruct((B, T // LANES, VT_ROWS, LANES), BF16),
        jax.ShapeDtypeStruct((B, T // TQ, GATE_ROWS, TQ), F32),
        jax.ShapeDtypeStruct((B, T, RET_WIDTH), BF16),
        jax.ShapeDtypeStruct((B, T, RET_WIDTH), BF16),
        jax.ShapeDtypeStruct((B, T, RET_WIDTH), BF16),
        jax.ShapeDtypeStruct((B, T, RET_WIDTH), F32),
    )
    vt_spec = pl.BlockSpec((1, nb, VT_ROWS, LANES), lambda b, i: (b, i, 0, 0))
    out_specs = (
        pl.BlockSpec((1, tm // TQ, LANES, NSA_HEADS * TQ), lambda b, i: (b, i, 0, 0)),
        pl.BlockSpec((1, 2, tm, KV_WIDTH), lambda b, i: (b, 0, i, 0)),
        row(KV_WIDTH), vt_spec, row(KV_WIDTH), vt_spec,
        pl.BlockSpec((1, tm // TQ, GATE_ROWS, TQ), lambda b, i: (b, i, 0, 0)),
        row(RET_WIDTH), row(RET_WIDTH), row(RET_WIDTH), row(RET_WIDTH),
    )
    return pl.pallas_call(
        _inproj_kernel,
        grid=(B, T // tm),
        in_specs=[pl.BlockSpec((1, tm, D), lambda b, i: (b, i, 0)),
                  _const_spec((1, D)), _const_spec(w_all.shape)],
        out_specs=out_specs,
        out_shape=out_shape,
        compiler_params=_cparams("parallel", "parallel"),
    )(x, pre_g.reshape(1, D), w_all)


def _compress_kernel(a_ref, pe_ref, w1_ref, w2_ref, rm_ref, t_ref, hi_scr):
    ncp = a_ref.shape[2]
    a = a_ref[0, 0].astype(F32)
    a_lo = (a + pe_ref[0, 0:1, :]).astype(BF16)
    a_hi = (a + pe_ref[0, 1:2, :]).astype(BF16)
    c = jnp.zeros((ncp, KV_WIDTH), F32)
    hi_scr[pl.ds(ncp, 8), :] = jnp.zeros((8, CMP_HIDDEN), F32)
    for g in range(NSA_KV_HEADS):
        p_lo = _mm(a_lo, w1_ref[0, 2 * g], preferred_element_type=F32)
        hi_scr[pl.ds(0, ncp), :] = _mm(a_hi, w1_ref[0, 2 * g + 1], preferred_element_type=F32)
        hdn = p_lo + hi_scr[pl.ds(1, ncp), :]
        hdn = (hdn * jax.nn.sigmoid(hdn)).astype(BF16)
        c = c + _mm(hdn, w2_ref[0, g], preferred_element_type=F32)
    rows = lax.broadcasted_iota(jnp.int32, c.shape, 0)
    c = jnp.where(rows < ncp - 1, c, 0.0)
    rm_ref[0, 0] = c.astype(BF16)
    for j in range(ncp // CH):
        t_ref[0, 0, j] = c[j * CH:(j + 1) * CH, :].T.astype(BF16)


def _compress_weights(cmp_pe, cmp_w1, cmp_w2):
    half = CMP_LEN // 2
    pe = cmp_pe.reshape(2, 2, half, 1, NSA_HEAD_DIM)
    pe = jnp.broadcast_to(pe, (2, 2, half, NSA_KV_HEADS, NSA_HEAD_DIM)).reshape(2, 2, half * KV_WIDTH)
    w1 = cmp_w1.reshape(2, 2, half, NSA_HEAD_DIM, CMP_HIDDEN)
    w1e = []
    for g in range(NSA_KV_HEADS):
        for hf in range(2):
            z = jnp.zeros((2, half, NSA_KV_HEADS, NSA_HEAD_DIM, CMP_HIDDEN), F32)
            z = z.at[:, :, g].set(w1[:, hf])
            w1e.append(z.reshape(2, half * KV_WIDTH, CMP_HIDDEN))
    w1e = jnp.stack(w1e, axis=1).astype(BF16)
    w2e = []
    for g in range(NSA_KV_HEADS):
        z = jnp.zeros((2, CMP_HIDDEN, NSA_KV_HEADS, NSA_HEAD_DIM), F32)
        z = z.at[:, :, g].set(cmp_w2)
        w2e.append(z.reshape(2, CMP_HIDDEN, KV_WIDTH))
    w2e = jnp.stack(w2e, axis=1).astype(BF16)
    return pe, w1e, w2e


def _compress(cmp, pe, w1e, w2e):
    B, _, T, _ = cmp.shape
    ncp = T // CMP_STRIDE
    a = cmp.reshape(B, 2, ncp, CMP_STRIDE * KV_WIDTH)
    kw = CMP_STRIDE * KV_WIDTH
    return pl.pallas_call(
        _compress_kernel,
        grid=(B, 2),
        in_specs=[pl.BlockSpec((1, 1, ncp, kw), lambda b, k: (b, k, 0, 0)),
                  pl.BlockSpec((1, 2, kw), lambda b, k: (k, 0, 0)),
                  pl.BlockSpec((1, 4, kw, CMP_HIDDEN), lambda b, k: (k, 0, 0, 0)),
                  pl.BlockSpec((1, 2, CMP_HIDDEN, KV_WIDTH), lambda b, k: (k, 0, 0, 0))],
        out_specs=(pl.BlockSpec((1, 1, ncp, KV_WIDTH), lambda b, k: (b, k, 0, 0)),
                   pl.BlockSpec((1, 1, ncp // CH, KV_WIDTH, CH), lambda b, k: (b, k, 0, 0, 0))),
        out_shape=(jax.ShapeDtypeStruct((B, 2, ncp, KV_WIDTH), BF16),
                   jax.ShapeDtypeStruct((B, 2, ncp // CH, KV_WIDTH, CH), BF16)),
        scratch_shapes=[pltpu.VMEM((ncp + 8, CMP_HIDDEN), F32)],
        compiler_params=_cparams("parallel", "parallel"),
    )(a, pe, w1e, w2e)


def _select_kernel(q_ref, kc_ref, vct_ref, ovl_ref, tab_ref, grp_ref, oc_ref, sel_ref, flag_ref,
                   s_scr, p_scr, imp_scr, oc_scr):
    tq = TQ
    nsel = ovl_ref.shape[1]
    q0 = pl.program_id(1) * tq
    qt = q_ref[0, 0]
    span = CH * CMP_STRIDE
    nch = (q0 + tq - CMP_LEN) // span + 1
    nfull = jnp.maximum((q0 - (span + CMP_LEN - CMP_STRIDE - 1)) // span + 1, 0)
    mrel = (lax.broadcasted_iota(jnp.int32, (CH, tq), 0) * CMP_STRIDE + (CMP_LEN - 1)
            - lax.broadcasted_iota(jnp.int32, (CH, tq), 1))
    heads = [slice(hh * tq, (hh + 1) * tq) for hh in range(NSA_HEADS)]

    def chunk_rows(c):
        return pl.ds(pl.multiple_of(c * CH, CH), CH)

    def scores(masked):
        def body(c, ms):
            rows = chunk_rows(c)
            s_c = _mm(kc_ref[0, 0, rows, :], qt, preferred_element_type=F32)
            ok = mrel <= q0 - c * span
            out = []
            for hh, sl in enumerate(heads):
                s = s_c[:, sl] + tab_ref[hh, rows, :]
                if masked:
                    s = jnp.where(ok, s, NEG)
                s_scr[rows, sl] = s
                out.append(jnp.maximum(ms[hh], jnp.max(s, axis=0, keepdims=True)))
            return tuple(out)
        return body

    m0 = tuple(jnp.full((1, tq), NEG, F32) for _ in heads)
    ms = lax.fori_loop(0, nfull, scores(False), m0)
    ms = lax.fori_loop(nfull, nch, scores(True), ms)

    def probs(masked):
        def body(c, dens):
            rows = chunk_rows(c)
            ok = mrel <= q0 - c * span
            out = []
            for hh, sl in enumerate(heads):
                p = jnp.exp2(s_scr[rows, sl] - ms[hh])
                if masked:
                    p = jnp.where(ok, p, 0.0)
                s_scr[rows, sl] = p
                out.append(dens[hh] + jnp.sum(p, axis=0, keepdims=True))
            return tuple(out)
        return body

    d0 = tuple(jnp.zeros((1, tq), F32) for _ in heads)
    dens = lax.fori_loop(0, nfull, probs(False), d0)
    dens = lax.fori_loop(nfull, nch, probs(True), dens)
    rden = [1.0 / jnp.maximum(d, 1e-30) for d in dens]

    imp_scr[...] = jnp.zeros_like(imp_scr)
    oc_scr[...] = jnp.zeros_like(oc_scr)

    def outputs(c, carry):
        rows = chunk_rows(c)
        for g in range(NSA_KV_HEADS):
            psum = jnp.zeros((CH, tq), F32)
            for r in range(NSA_GROUP):
                hh = g * NSA_GROUP + r
                pn = s_scr[rows, heads[hh]] * rden[hh]
                psum = psum + pn
                p_scr[:, heads[hh]] = pn.astype(BF16)
            p_hi = psum.astype(BF16)
            p_lo = (psum - p_hi.astype(F32)).astype(BF16)
            imp_scr[g] += (_mm(ovl_ref[c], p_hi, preferred_element_type=F32)
                           + _mm(ovl_ref[c], p_lo, preferred_element_type=F32))
        oc_scr[...] += _mm(vct_ref[0, 0, c], p_scr[...], preferred_element_type=F32)
        return carry

    lax.fori_loop(0, nch, outputs, 0)
    for hh, sl in enumerate(heads):
        oc_ref[0, 0, hh] = oc_scr[pl.ds((hh // NSA_GROUP) * NSA_HEAD_DIM, NSA_HEAD_DIM), sl]

    n_top = min(SLC_TOPN, nsel)
    n_seen = (q0 + tq - 1) // SLC_LEN + 1
    sizes = sorted({r for r in (nsel * i // 8 for i in range(1, 9)) if r >= n_top and r % 8 == 0})

    def select(rows):
        jb = lax.broadcasted_iota(jnp.int32, (rows, tq), 0)
        cur = (q0 + lax.broadcasted_iota(jnp.int32, (rows, tq), 1)) // SLC_LEN
        forced = (jb == 0) | (jb == cur) | (jb == cur - 1)
        cur1 = (q0 + lax.broadcasted_iota(jnp.int32, (1, tq), 1)) // SLC_LEN
        quota = n_top - (1 + (cur1 >= 1).astype(jnp.int32) + (cur1 >= 2).astype(jnp.int32))
        imps = tuple(jnp.where(forced, TAKEN, jnp.where(jb <= cur, imp_scr[g, pl.ds(0, rows), :], -1.0))
                     for g in range(NSA_KV_HEADS))

        def pick(i, imps):
            out = []
            for imp in imps:
                best = jnp.max(imp, axis=0, keepdims=True)
                first = jnp.min(jnp.where(imp == best, jb, rows), axis=0, keepdims=True)
                first = jnp.where(i < quota, first, -1)
                out.append(jnp.where(jb == first, TAKEN, imp))
            return tuple(out)

        rounds = n_top - jnp.where(q0 >= 2 * SLC_LEN, 3, 1)
        imps = lax.fori_loop(0, rounds, pick, imps)
        ones = jnp.ones((ONES_ROWS, tq), BF16)
        cnt = jnp.zeros((ONES_ROWS, rows), F32)
        for g in range(NSA_KV_HEADS):
            taken = imps[g] < -2.0
            sel_ref[0, 0, g,pl.ds(0, rows), :] = jnp.where(taken, 0.0, NEG)
            if rows < nsel:
                sel_ref[0, 0, g,pl.ds(rows, nsel - rows), :] = jnp.full((nsel - rows, tq), NEG, F32)
            cnt = cnt + _mmg(ones, jnp.where(taken, 1.0, 0.0).astype(BF16), _NT,
                             preferred_element_type=F32)
        act = _mm(jnp.minimum(cnt, 1.0).astype(BF16), grp_ref[pl.ds(0, rows), :],
                  preferred_element_type=F32)
        flag_ref[0, 0] = (act[0:1, :] > 0.0).astype(jnp.int32)

    lo = 0
    for rows in sizes:
        cond = n_seen > lo
        if rows < nsel:
            cond = cond & (n_seen <= rows)
        pl.when(cond)(functools.partial(select, rows))
        lo = rows


def _overlap_t(T):
    ncp = T // CMP_STRIDE
    nc = (T - CMP_LEN) // CMP_STRIDE + 1
    nsel = T // SLC_LEN
    c_start = np.arange(nc) * CMP_STRIDE
    c_end = c_start + CMP_LEN - 1
    s_start = np.arange(nsel) * SLC_LEN
    s_end = s_start + SLC_LEN - 1
    ov = np.clip(np.minimum(c_end[:, None], s_end[None, :])
                 - np.maximum(c_start[:, None], s_start[None, :]) + 1, 0, None) / CMP_LEN
    out = np.zeros((nsel, ncp), np.float32)
    out[:, :nc] = ov.T
    out = out.reshape(nsel, ncp // CH, CH).transpose(1, 0, 2)
    return jnp.asarray(out, dtype=BF16)


def _select_tables(T):
    ncp = T // CMP_STRIDE
    nsel = T // SLC_LEN
    nkt = T // min(TK, T)
    n = np.arange(ncp, dtype=np.float64)[:, None]
    l = np.arange(TQ, dtype=np.float64)[None, :]
    dist = l - (n * CMP_STRIDE + (CMP_LEN - 1) / 2.0)
    tab = np.stack([-a * dist for a in ALIBI2]).astype(np.float32)
    grp = (np.arange(nsel)[:, None] // (nsel // nkt) == np.arange(nkt)[None, :]).astype(np.float32)
    grp[0, :] = 0.0
    return jnp.asarray(tab), jnp.asarray(grp, dtype=BF16)


def _select(q_hm, c_rm, c_t):
    B, T = q_hm.shape[0], q_hm.shape[1] * TQ
    ncp = T // CMP_STRIDE
    nsel = T // SLC_LEN
    nqt = T // TQ
    nkt = T // min(TK, T)
    tab, grp = _select_tables(T)
    return pl.pallas_call(
        _select_kernel,
        grid=(B, nqt),
        in_specs=[pl.BlockSpec((1, 1, LANES, NSA_HEADS * TQ), lambda b, i: (b, i, 0, 0)),
                  pl.BlockSpec((1, 1, ncp, KV_WIDTH), lambda b, i: (b, 0, 0, 0)),
                  pl.BlockSpec((1, 1, ncp // CH, KV_WIDTH, CH), lambda b, i: (b, 1, 0, 0, 0)),
                  _const_spec((ncp // CH, nsel, CH)), _const_spec(tab.shape), _const_spec(grp.shape)],
        out_specs=(pl.BlockSpec((1, 1, NSA_HEADS, NSA_HEAD_DIM, TQ), lambda b, i: (b, i, 0, 0, 0)),
                   pl.BlockSpec((1, 1, NSA_KV_HEADS, nsel, TQ), lambda b, i: (b, i, 0, 0, 0)),
                   pl.BlockSpec((1, 1, 1, nkt), lambda b, i: (b, i, 0, 0))),
        out_shape=(jax.ShapeDtypeStruct((B, nqt, NSA_HEADS, NSA_HEAD_DIM, TQ), F32),
                   jax.ShapeDtypeStruct((B, nqt, NSA_KV_HEADS, nsel, TQ), F32),
                   jax.ShapeDtypeStruct((B, nqt, 1, nkt), jnp.int32)),
        scratch_shapes=[pltpu.VMEM((ncp, NSA_HEADS * TQ), F32),
                        pltpu.VMEM((CH, NSA_HEADS * TQ), BF16),
                        pltpu.VMEM((NSA_KV_HEADS, nsel, TQ), F32),
                        pltpu.VMEM((KV_WIDTH, NSA_HEADS * TQ), F32)],
        compiler_params=_cparams("parallel", "arbitrary"),
    )(q_hm, c_rm, c_t, _overlap_t(T), tab, grp)


def _attn_kernel(flags_ref, q_ref, ks_ref, vst_ref, kw_ref, vwt_ref, sel_ref, oc_ref, gt_ref,
                 gain_ref, kf_ref, kh_ref, qf_ref, wb_ref, o_ref, acc_scr, p_scr, m_scr, alpha_scr,
                 pw_scr, ow_scr, qf_scr, act_scr):
    tq = TQ
    T = ks_ref.shape[1]
    qi = pl.program_id(1)
    q0 = qi * tq
    tk = kf_ref.shape[0]
    wk = wb_ref.shape[1]
    nkt = T // tk
    bpt = tk // SLC_LEN
    vpt = tk // LANES
    assert bpt == 8
    qt = q_ref[0, 0]
    heads = [slice(hh * tq, (hh + 1) * tq) for hh in range(NSA_HEADS)]

    qf_scr[pl.ds(0, LANES), :] = qt
    qf_scr[pl.ds(LANES, LANES), :] = qf_ref[...]
    row8 = lax.broadcasted_iota(jnp.int32, (8, tq), 0)

    def digits3(v):
        d1 = v.astype(BF16).astype(F32)
        d2 = (v - d1).astype(BF16).astype(F32)
        return d1, d2, v - d1 - d2

    def tile(kt, rows, first, head=False):
        k0 = kt * tk
        shift = (q0 - k0).astype(F32)
        for g in range(NSA_KV_HEADS):
            sb = sel_ref[0, 0, g,pl.ds(pl.multiple_of(kt * bpt, bpt), bpt), :]
            for r in range(NSA_GROUP):
                hh = g * NSA_GROUP + r
                d = digits3(jnp.full((8, tq), -ALIBI2[hh], F32) * shift)
                low = jnp.where(row8 == 0, d[0], jnp.where(row8 == 1, d[1], jnp.where(row8 == 2, d[2], 0.0)))
                if head:
                    e = digits3(jnp.full((8, tq), -ALIBI2[hh], F32) * q0.astype(F32))
                    low = jnp.where(row8 == 3, e[0], jnp.where(row8 == 4, e[1], jnp.where(row8 == 5, e[2], low)))
                qf_scr[pl.ds(LANES, 16), heads[hh]] = jnp.concatenate([sb, low], axis=0).astype(BF16)
                if head:
                    qf_scr[pl.ds(LANES + 16, 16), heads[hh]] = jnp.concatenate(
                        [sel_ref[0, 0, g,pl.ds(0, 8), :], jnp.zeros((8, tq), F32)], axis=0).astype(BF16)
        lhs = jnp.concatenate([ks_ref[0, pl.ds(pl.multiple_of(k0, tk), rows), :],
                               kf_ref[pl.ds(0, rows), :]], axis=1)
        vts = [vst_ref[0, kt * vpt + j] for j in range(rows // LANES)]
        if head:
            lhs = jnp.concatenate(
                [jnp.concatenate([ks_ref[0, pl.ds(0, LANES), :], kh_ref[...]], axis=1), lhs], axis=0)
            vts = [vst_ref[0, 0]] + vts
            rows = rows + LANES
        vt = jnp.concatenate(vts, axis=1)
        if first:
            c_i = lax.broadcasted_iota(jnp.int32, (rows, tq), 0)
            l_i = lax.broadcasted_iota(jnp.int32, (rows, tq), 1)
            if head:
                causal = (c_i < LANES) | (c_i + (k0 - q0 - LANES) <= l_i)
            else:
                causal = c_i + (k0 - q0) <= l_i
        s_all = _mm(lhs, qf_scr[...], preferred_element_type=F32)
        for hh, sl in enumerate(heads):
            s = s_all[:, sl]
            if first:
                s = jnp.where(causal, s, NEG)
                m_new = jnp.max(s, axis=0, keepdims=True)
            else:
                m_old = m_scr[:, sl]
                m_new = jnp.maximum(m_old, jnp.max(s, axis=0, keepdims=True))
                alpha_scr[:, sl] = jnp.exp2(m_old - m_new)
            m_scr[:, sl] = m_new
            p_scr[pl.ds(0, rows), sl] = jnp.exp2(s - m_new).astype(BF16)
        pv = _mm(vt, p_scr[pl.ds(0, rows), :], preferred_element_type=F32)
        if first:
            acc_scr[...] = pv
        else:
            acc_scr[...] = acc_scr[...] * alpha_scr[...] + pv

    kt_diag = q0 // tk
    fbase = (pl.program_id(0) * pl.num_programs(1) + qi) * nkt
    with_head = (kt_diag > 0) & (flags_ref[fbase] == 0)
    for jj in range(tk // tq):
        here = qi % (tk // tq) == jj
        pl.when(here & with_head)(functools.partial(tile, kt_diag, (jj + 1) * tq, True, True))
        pl.when(here & jnp.logical_not(with_head))(
            functools.partial(tile, kt_diag, (jj + 1) * tq, True, False))

    def scan(kt, n):
        act_scr[n] = kt
        return n + (flags_ref[fbase + kt] != 0).astype(jnp.int32)

    n_act = lax.fori_loop(0, kt_diag, scan, 0)

    def body(i, carry):
        tile(act_scr[i], tk, False)
        return carry

    lax.fori_loop(0, n_act, body, 0)

    def window(ws, general):
        sw_all = _mm(kw_ref[0, pl.ds(pl.multiple_of(ws, LANES), wk), :], qt,
                     preferred_element_type=F32)
        vwt = jnp.concatenate([vwt_ref[0, ws // LANES + j] for j in range(wk // LANES)], axis=1)
        if general:
            kpos = ws + lax.broadcasted_iota(jnp.int32, (wk, tq), 0)
            dist = q0 + lax.broadcasted_iota(jnp.int32, (wk, tq), 1) - kpos
            wmask = (dist >= 0) & (dist < WINDOW)
            distf = dist.astype(F32)
        for hh, sl in enumerate(heads):
            if general:
                s = jnp.where(wmask, sw_all[:, sl] - ALIBI2[hh] * distf, NEG)
            else:
                s = sw_all[:, sl] + wb_ref[hh]
            pw_scr[:, sl] = jnp.exp2(s - jnp.max(s, axis=0, keepdims=True)).astype(BF16)
        ow_scr[...] = _mm(vwt, pw_scr[...], preferred_element_type=F32)

    @pl.when(q0 >= WINDOW)
    def _():
        window(q0 - WINDOW, False)

    @pl.when(q0 < WINDOW)
    def _():
        window(q0 * 0, True)

    gs = jax.nn.sigmoid(gt_ref[0, 0])
    outs = []
    ssq = jnp.zeros((1, tq), F32)
    for hh in range(NSA_HEADS):
        sl = heads[hh]
        ch = pl.ds((hh // NSA_GROUP) * NSA_HEAD_DIM, NSA_HEAD_DIM)
        o_s = acc_scr[ch, sl] / jnp.maximum(acc_scr[KV_WIDTH:KV_WIDTH + 1, sl], 1e-30)
        o_w = ow_scr[ch, sl] / jnp.maximum(ow_scr[KV_WIDTH:KV_WIDTH + 1, sl], 1e-30)
        o = (gs[3 * hh:3 * hh + 1, :] * oc_ref[0, 0, hh] + gs[3 * hh + 1:3 * hh + 2, :] * o_s
             + gs[3 * hh + 2:3 * hh + 3, :] * o_w)
        ssq = ssq + jnp.sum(o * o, axis=0, keepdims=True)
        outs.append(o)
    rinv = lax.rsqrt(ssq / NSA_WIDTH + EPS)
    for j in range(NSA_HEADS // 2):
        pair = jnp.concatenate([outs[2 * j] * rinv * gain_ref[2 * j],
                                outs[2 * j + 1] * rinv * gain_ref[2 * j + 1]], axis=0)
        o_ref[0, :, j * LANES:(j + 1) * LANES] = pair.T.astype(BF16)


def _attn_gain(nsa_out_g):
    g = nsa_out_g.reshape(NSA_HEADS, NSA_HEAD_DIM)
    return jnp.broadcast_to(g[:, :, None], (NSA_HEADS, NSA_HEAD_DIM, TQ))


def _bf16_digits(x, n):
    out = []
    for _ in range(n):
        d = float(np.asarray(x, np.float32).astype(jnp.bfloat16).astype(np.float32))
        out.append(d)
        x = x - d
    return out


def _attn_tables(T):
    tk = min(TK, T)
    wk = min(WKEYS, T)
    c = np.arange(tk)
    kf = np.zeros((tk, LANES), np.float32)
    kf[c, c // SLC_LEN] = 1.0
    kf[:, 8:11] = 1.0
    kf[:, 32:35] = (c // 32)[:, None]
    kf[:, 35:38] = (c % 32)[:, None]
    ch = np.arange(LANES)
    kh = np.zeros((LANES, LANES), np.float32)
    kh[ch, 16 + ch // SLC_LEN] = 1.0
    kh[:, 11:14] = 1.0
    kh[:, 32:35] = (ch // 32)[:, None]
    kh[:, 35:38] = (ch % 32)[:, None]
    qf = np.zeros((LANES, NSA_HEADS * TQ), np.float32)
    for hh, a in enumerate(ALIBI2):
        d = _bf16_digits(a, 3)
        for i in range(3):
            qf[32 + i, hh * TQ:(hh + 1) * TQ] = 32.0 * d[i]
            qf[35 + i, hh * TQ:(hh + 1) * TQ] = d[i]
    l = np.arange(TQ, dtype=np.float64)[None, :]
    dist = WINDOW + l - np.arange(wk, dtype=np.float64)[:, None]
    inside = (dist >= 0) & (dist < WINDOW)
    wb = np.stack([np.where(inside, -a * dist, NEG) for a in ALIBI2]).astype(np.float32)
    return (jnp.asarray(kf, dtype=BF16), jnp.asarray(kh, dtype=BF16), jnp.asarray(qf, dtype=BF16),
            jnp.asarray(wb))


def _attn(q_hm, ks, vst, kw, vwt, sel, flags, oc, gt, gain_b):
    B, T = q_hm.shape[0], q_hm.shape[1] * TQ
    nsel = T // SLC_LEN
    nqt = T // TQ
    kf, kh, qf, wb = _attn_tables(T)
    tk, wk = kf.shape[0], wb.shape[1]
    once = dict(pipeline_mode=pl.Buffered(1))
    full_k = pl.BlockSpec((1, T, KV_WIDTH), lambda b, i, f: (b, 0, 0), **once)
    full_vt = pl.BlockSpec((1, T // LANES, VT_ROWS, LANES), lambda b, i, f: (b, 0, 0, 0), **once)
    const = lambda shape: pl.BlockSpec(shape, lambda b, i, f: (0,) * len(shape), **once)
    grid_spec = pltpu.PrefetchScalarGridSpec(
        num_scalar_prefetch=1,
        grid=(B, nqt),
        in_specs=[pl.BlockSpec((1, 1, LANES, NSA_HEADS * TQ), lambda b, i, f: (b, i, 0, 0)),
                  full_k, full_vt, full_k, full_vt,
                  pl.BlockSpec((1, 1, NSA_KV_HEADS, nsel, TQ), lambda b, i, f: (b, i, 0, 0, 0)),
                  pl.BlockSpec((1, 1, NSA_HEADS, NSA_HEAD_DIM, TQ), lambda b, i, f: (b, i, 0, 0, 0)),
                  pl.BlockSpec((1, 1, GATE_ROWS, TQ), lambda b, i, f: (b, i, 0, 0)),
                  const((NSA_HEADS, NSA_HEAD_DIM, TQ)), const(kf.shape), const(kh.shape),
                  const(qf.shape), const(wb.shape)],
        out_specs=pl.BlockSpec((1, TQ, NSA_WIDTH), lambda b, i, f: (b, i, 0)),
        scratch_shapes=[pltpu.VMEM((VT_ROWS, NSA_HEADS * TQ), F32),
                        pltpu.VMEM((tk + LANES, NSA_HEADS * TQ), BF16),
                        pltpu.VMEM((1, NSA_HEADS * TQ), F32),
                        pltpu.VMEM((1, NSA_HEADS * TQ), F32),
                        pltpu.VMEM((wk, NSA_HEADS * TQ), BF16),
                        pltpu.VMEM((VT_ROWS, NSA_HEADS * TQ), F32),
                        pltpu.VMEM((2 * LANES, NSA_HEADS * TQ), BF16),
                        pltpu.SMEM((T // tk,), jnp.int32)])
    return pl.pallas_call(
        _attn_kernel,
        grid_spec=grid_spec,
        out_shape=jax.ShapeDtypeStruct((B, T, NSA_WIDTH), BF16),
        compiler_params=_cparams("parallel", "arbitrary"),
    )(flags.reshape(-1), q_hm, ks, vst, kw, vwt, sel, oc, gt, gain_b, kf, kh, qf, wb)


def _retention_kernel(q_ref, k_ref, v_ref, g_ref, dm_ref, qd_ref, kd_ref, gn_ref, o_ref, st_scr,
                      *, chunk_dec):
    rt = q_ref.shape[1]
    C, d = RET_CHUNK, RET_HEAD_DIM

    @pl.when(pl.program_id(1) == 0)
    def _():
        st_scr[...] = jnp.zeros_like(st_scr)

    for c in range(rt // C):
        rows = slice(c * C, (c + 1) * C)
        for h in range(RET_HEADS):
            cols = slice(h * d, (h + 1) * d)
            q = q_ref[0, rows, cols]
            k = k_ref[0, rows, cols]
            v = v_ref[0, rows, cols]
            state = st_scr[h]
            s = _mmg(q, k, _NT, preferred_element_type=F32) * dm_ref[h]
            inner = _mm(s.astype(BF16), v, preferred_element_type=F32)
            cross = _mm(q, state.astype(BF16), preferred_element_type=F32) * qd_ref[h]
            kd = (k.astype(F32) * kd_ref[h]).astype(BF16)
            st_scr[h] = state * chunk_dec[h] + _mmg(kd, v, _TN, preferred_element_type=F32)
            o = inner + cross
            mu = jnp.mean(o, axis=-1, keepdims=True)
            oc = o - mu
            var = jnp.mean(oc * oc, axis=-1, keepdims=True)
            o = oc * lax.rsqrt(var + EPS) * gn_ref[:, cols]
            gg = g_ref[0, rows, cols]
            o_ref[0, rows, cols] = (o * (gg * jax.nn.sigmoid(gg))).astype(BF16)


def _retention_consts():
    C, d = RET_CHUNK, RET_HEAD_DIM
    log_g = np.log(1.0 - np.exp2(-5.0 - np.arange(RET_HEADS, dtype=np.float64)))
    pos = np.arange(C, dtype=np.float64)
    diff = pos[:, None] - pos[None, :]
    scale = d ** -0.5
    dmask = np.where(diff >= 0, np.exp(np.maximum(diff, 0.0) * log_g[:, None, None]), 0.0) * scale
    q_dec = np.exp((pos + 1.0) * log_g[:, None])
    k_dec = np.exp((C - 1.0 - pos) * log_g[:, None]) * scale
    chunk_dec = tuple(float(x) for x in np.exp(C * log_g))
    bc = lambda a: jnp.asarray(np.broadcast_to(a[:, :, None], (RET_HEADS, C, d)).astype(np.float32))
    return jnp.asarray(dmask.astype(np.float32)), bc(q_dec), bc(k_dec), chunk_dec


def _retention(qr, kr, vr, gr, gn_gain):
    B, T, W = qr.shape
    rt = min(2 * TM, T)
    dmask, qd, kd, chunk_dec = _retention_consts()
    row = pl.BlockSpec((1, rt, W), lambda b, i: (b, i, 0))
    cst = _const_spec((RET_HEADS, RET_CHUNK, RET_HEAD_DIM))
    return pl.pallas_call(
        functools.partial(_retention_kernel, chunk_dec=chunk_dec),
        grid=(B, T // rt),
        in_specs=[row, row, row, row, _const_spec(dmask.shape), cst, cst, _const_spec((1, W))],
        out_specs=row,
        out_shape=jax.ShapeDtypeStruct((B, T, W), BF16),
        scratch_shapes=[pltpu.VMEM((RET_HEADS, RET_HEAD_DIM, RET_HEAD_DIM), F32)],
        compiler_params=_cparams("parallel", "arbitrary"),
    )(qr, kr, vr, gr, dmask, qd, kd, gn_gain.reshape(1, W))


def _outproj_kernel(x_ref, on_ref, or_ref, wn_ref, wr_ref, post_ref, o_ref):
    y = (_mm(on_ref[0], wn_ref[...], preferred_element_type=F32)
         + _mm(or_ref[0], wr_ref[...], preferred_element_type=F32))
    o_ref[0] = x_ref[0] + _rms(y, post_ref[...])


def _outproj_weights(w_out):
    return w_out[:NSA_WIDTH].astype(BF16), w_out[NSA_WIDTH:].astype(BF16)


def _outproj(x, o_nsa, o_ret, wn, wr, post_g):
    B, T, D = x.shape
    tm = min(2 * TM, T)
    row = lambda w: pl.BlockSpec((1, tm, w), lambda b, i: (b, i, 0))
    return pl.pallas_call(
        _outproj_kernel,
        grid=(B, T // tm),
        in_specs=[row(D), row(NSA_WIDTH), row(RET_WIDTH), _const_spec(wn.shape), _const_spec(wr.shape),
                  _const_spec((1, D))],
        out_specs=row(D),
        out_shape=jax.ShapeDtypeStruct(x.shape, F32),
        compiler_params=_cparams("parallel", "parallel"),
    )(x, o_nsa, o_ret, wn, wr, post_g.reshape(1, D))


def _memkv_kernel(m_ref, g_ref, wk_ref, wv_ref, k_ref, v_ref):
    h = _rms(m_ref[0], g_ref[...]).astype(BF16)
    k_ref[0] = _mm(h, wk_ref[...], preferred_element_type=F32).astype(BF16)
    v_ref[0] = _mm(h, wv_ref[...], preferred_element_type=F32).astype(BF16)


def _memkv(mem, g, wk, wv):
    B, M, D = mem.shape
    blk = pl.BlockSpec((1, M, D), lambda b: (b, 0, 0))
    return pl.pallas_call(
        _memkv_kernel,
        grid=(B,),
        in_specs=[blk, _const_spec((1, D)), _const_spec(wk.shape), _const_spec(wv.shape)],
        out_specs=(blk, blk),
        out_shape=(jax.ShapeDtypeStruct(mem.shape, BF16), jax.ShapeDtypeStruct(mem.shape, BF16)),
        compiler_params=_cparams("parallel"),
    )(mem, g.reshape(1, D), wk, wv)


def _xattn_kernel(x_ref, pre_ref, wq_ref, k_ref, v_ref, wo_ref, post_ref, o_ref):
    x = x_ref[0]
    h = _rms(x, pre_ref[...]).astype(BF16)
    q = (_mm(h, wq_ref[...], preferred_element_type=F32) * (MEM_HEAD_DIM ** -0.5)).astype(BF16)
    heads = []
    for hd in range(MEM_HEADS):
        cols = slice(hd * MEM_HEAD_DIM, (hd + 1) * MEM_HEAD_DIM)
        s = _mmg(q[:, cols], k_ref[0, :, cols], _NT, preferred_element_type=F32)
        p = jnp.exp(s - jnp.max(s, axis=-1, keepdims=True))
        den = jnp.sum(p, axis=-1, keepdims=True)
        oh = _mm(p.astype(BF16), v_ref[0, :, cols], preferred_element_type=F32) / den
        heads.append(oh.astype(BF16))
    o = jnp.concatenate(heads, axis=1)
    y = _mm(o, wo_ref[...], preferred_element_type=F32)
    o_ref[0] = x + _rms(y, post_ref[...])


def _xattn(x, pre_g, wq, km, vm, wo, post_g):
    B, T, D = x.shape
    M = km.shape[1]
    tm = min(2 * TM, T)
    row = pl.BlockSpec((1, tm, D), lambda b, i: (b, i, 0))
    kv = pl.BlockSpec((1, M, D), lambda b, i: (b, 0, 0))
    return pl.pallas_call(
        _xattn_kernel,
        grid=(B, T // tm),
        in_specs=[row, _const_spec((1, D)), _const_spec(wq.shape), kv, kv, _const_spec(wo.shape),
                  _const_spec((1, D))],
        out_specs=row,
        out_shape=jax.ShapeDtypeStruct(x.shape, F32),
        compiler_params=_cparams("parallel", "parallel"),
    )(x, pre_g.reshape(1, D), wq, km, vm, wo, post_g.reshape(1, D))


def _hybrid_mixer(x, pre_g, w_in, cmp_pe, cmp_w1, cmp_w2, nsa_out_g, ret_gn_g, w_out, post_g):
    q_hm, cmp, ks, vst, kw, vwt, gt, qr, kr, vr, gr = _inproj(x, pre_g, _inproj_weight(w_in))
    c_rm, c_t = _compress(cmp, *_compress_weights(cmp_pe, cmp_w1, cmp_w2))
    oc, sel, flags = _select(q_hm, c_rm, c_t)
    o_nsa = _attn(q_hm, ks, vst, kw, vwt, sel, flags, oc, gt, _attn_gain(nsa_out_g))
    o_ret = _retention(qr, kr, vr, gr, ret_gn_g)
    wn, wr = _outproj_weights(w_out)
    return _outproj(x, o_nsa, o_ret, wn, wr, post_g)


def kernel(x, mem, ffn1_pre_g, ffn1_w_gate, ffn1_w_up, ffn1_w_down, ffn1_post_g, mix_pre_g, w_in, cmp_pe, cmp_w1, cmp_w2, nsa_out_g, ret_gn_g, w_out, mix_post_g, xa_pre_g, xa_mem_g, xa_wq, xa_wk, xa_wv, xa_wo, xa_post_g, ffn2_pre_g, ffn2_w_gate, ffn2_w_up, ffn2_w_down, ffn2_post_g):
    depth = w_in.shape[0]
    bf = lambda w: w.astype(BF16)
    for l in range(depth):
        x = _ffn(x, ffn1_pre_g[l], bf(ffn1_w_gate[l]), bf(ffn1_w_up[l]), bf(ffn1_w_down[l]),
                 ffn1_post_g[l])
        x = _hybrid_mixer(x, mix_pre_g[l], w_in[l], cmp_pe[l], cmp_w1[l], cmp_w2[l], nsa_out_g[l],
                          ret_gn_g[l], w_out[l], mix_post_g[l])
        km, vm = _memkv(mem, xa_mem_g[l], bf(xa_wk[l]), bf(xa_wv[l]))
        x = _xattn(x, xa_pre_g[l], bf(xa_wq[l]), km, vm, bf(xa_wo[l]), xa_post_g[l])
        x = _ffn(x, ffn2_pre_g[l], bf(ffn2_w_gate[l]), bf(ffn2_w_up[l]), bf(ffn2_w_down[l]),
                 ffn2_post_g[l])
    return x
```

```python
import functools

import numpy as np
import jax
import jax.numpy as jnp
from jax import lax
from jax.experimental import pallas as pl
from jax.experimental.pallas import tpu as pltpu

F32 = jnp.float32
BF16 = jnp.bfloat16

D_MODEL = 1024
NSA_HEADS = 8
NSA_HEAD_DIM = 64
NSA_KV_HEADS = 2
NSA_GROUP = NSA_HEADS // NSA_KV_HEADS
NSA_WIDTH = NSA_HEADS * NSA_HEAD_DIM
KV_WIDTH = NSA_KV_HEADS * NSA_HEAD_DIM
CMP_LEN = 32
CMP_STRIDE = 16
CMP_HIDDEN = 256
SLC_LEN = 64
SLC_TOPN = 16
WINDOW = 512
RET_HEADS = 4
RET_HEAD_DIM = 128
RET_WIDTH = RET_HEADS * RET_HEAD_DIM
RET_CHUNK = 256
MEM_HEADS = 4
MEM_HEAD_DIM = D_MODEL // MEM_HEADS
D_FF = 2816
EPS = 1e-6
NEG = -1e30
TAKEN = -3e38
GATE_ROWS = 32

IN_SIZES = (NSA_WIDTH, KV_WIDTH, KV_WIDTH, KV_WIDTH, KV_WIDTH, KV_WIDTH, KV_WIDTH,
            NSA_HEADS * 3, RET_WIDTH, RET_WIDTH, RET_WIDTH, RET_WIDTH)
IN_OFFSETS = tuple(int(o) for o in np.cumsum(IN_SIZES)[:-1])

LANES = 128
LOG2E = 1.4426950408889634
ALIBI2 = tuple(float(2.0 ** (-8.0 * i / NSA_HEADS)) * LOG2E for i in range(1, NSA_HEADS + 1))
ONES_ROWS = 16
VT_ROWS = KV_WIDTH + ONES_ROWS
CH = 256

TM = 512
TQ = 128
TK = 512
WKEYS = WINDOW + TQ
VMEM_LIMIT = 56 * 1024 * 1024

_NT = (((1,), (1,)), ((), ()))
_TN = (((0,), (0,)), ((), ()))


def _mm(a, b, preferred_element_type=F32):
    return jnp.dot(a, b, preferred_element_type=preferred_element_type)


def _mmg(a, b, dims, preferred_element_type=F32):
    return lax.dot_general(a, b, dims, preferred_element_type=preferred_element_type)


def _cparams(*sem):
    return pltpu.CompilerParams(dimension_semantics=sem, vmem_limit_bytes=VMEM_LIMIT)


def _rms(x, g):
    return x * lax.rsqrt(jnp.mean(x * x, axis=-1, keepdims=True) + EPS) * g


def _const_spec(shape):
    nd = len(shape)
    return pl.BlockSpec(shape, lambda *_: (0,) * nd, pipeline_mode=pl.Buffered(1))


def _ffn_kernel(x_ref, pre_ref, wg_ref, wu_ref, wd_ref, post_ref, o_ref):
    x = x_ref[0]
    h = _rms(x, pre_ref[...]).astype(BF16)
    g = _mm(h, wg_ref[...], preferred_element_type=F32)
    u = _mm(h, wu_ref[...], preferred_element_type=F32)
    a = (g * jax.nn.sigmoid(g) * u).astype(BF16)
    y = _mm(a, wd_ref[...], preferred_element_type=F32)
    o_ref[0] = x + 0.5 * _rms(y, post_ref[...])


def _ffn(x, pre_g, wg, wu, wd, post_g):
    B, T, D = x.shape
    tm = min(TM, T)
    return pl.pallas_call(
        _ffn_kernel,
        grid=(B, T // tm),
        in_specs=[pl.BlockSpec((1, tm, D), lambda b, i: (b, i, 0)),
                  _const_spec((1, D)), _const_spec(wg.shape), _const_spec(wu.shape),
                  _const_spec(wd.shape), _const_spec((1, D))],
        out_specs=pl.BlockSpec((1, tm, D), lambda b, i: (b, i, 0)),
        out_shape=jax.ShapeDtypeStruct(x.shape, F32),
        compiler_params=_cparams("parallel", "parallel"),
    )(x, pre_g.reshape(1, D), wg, wu, wd, post_g.reshape(1, D))


_C_Q = 0
_C_CMP = _C_Q + NSA_WIDTH
_C_KS = _C_CMP + 2 * KV_WIDTH
_C_VS = _C_KS + KV_WIDTH
_C_KW = _C_VS + KV_WIDTH
_C_VW = _C_KW + KV_WIDTH
_C_GATE = _C_VW + KV_WIDTH
_C_QR = _C_GATE + LANES
_C_KR = _C_QR + RET_WIDTH
_C_VR = _C_KR + RET_WIDTH
_C_GR = _C_VR + RET_WIDTH
_C_END = _C_GR + RET_WIDTH


def _inproj_kernel(x_ref, pre_ref, w_ref, q_ref, cmp_ref, ks_ref, vst_ref, kw_ref, vwt_ref,
                   gt_ref, qr_ref, kr_ref, vr_ref, gr_ref):
    tm = x_ref.shape[1]
    h = _rms(x_ref[0], pre_ref[...]).astype(BF16)

    def proj(lo, hi):
        return _mm(h, w_ref[:, lo:hi], preferred_element_type=F32)

    q = proj(_C_Q, _C_CMP)
    zero = jnp.zeros((NSA_HEAD_DIM, TQ), F32)
    for j in range(tm // TQ):
        for pair in range(NSA_HEADS // 2):
            t = q[j * TQ:(j + 1) * TQ, pair * LANES:(pair + 1) * LANES].T
            for k in range(2):
                hh = 2 * pair + k
                dims = t[k * NSA_HEAD_DIM:(k + 1) * NSA_HEAD_DIM, :]
                slab = [dims, zero] if hh // NSA_GROUP == 0 else [zero, dims]
                q_ref[0, j, :, hh * TQ:(hh + 1) * TQ] = jnp.concatenate(slab, axis=0).astype(BF16)
    c = proj(_C_CMP, _C_KS)
    cmp_ref[0, 0] = c[:, :KV_WIDTH].astype(BF16)
    cmp_ref[0, 1] = c[:, KV_WIDTH:].astype(BF16)
    ks_ref[0] = proj(_C_KS, _C_VS).astype(BF16)
    kw_ref[0] = proj(_C_KW, _C_VW).astype(BF16)
    vs = proj(_C_VS, _C_KW)
    vw = proj(_C_VW, _C_GATE)
    ones = jnp.ones((ONES_ROWS, LANES), BF16)
    for j in range(tm // LANES):
        vst_ref[0, j, :KV_WIDTH, :] = vs[j * LANES:(j + 1) * LANES, :].T.astype(BF16)
        vwt_ref[0, j, :KV_WIDTH, :] = vw[j * LANES:(j + 1) * LANES, :].T.astype(BF16)
        vst_ref[0, j, KV_WIDTH:, :] = ones
        vwt_ref[0, j, KV_WIDTH:, :] = ones
    gates_t = proj(_C_GATE, _C_QR).T
    for j in range(tm // TQ):
        gt_ref[0, j] = gates_t[:GATE_ROWS, j * TQ:(j + 1) * TQ]
    qr_ref[0] = proj(_C_QR, _C_KR).astype(BF16)
    kr_ref[0] = proj(_C_KR, _C_VR).astype(BF16)
    vr_ref[0] = proj(_C_VR, _C_GR).astype(BF16)
    gr_ref[0] = proj(_C_GR, _C_END)


def _inproj_weight(w_in):
    parts = jnp.split(w_in, IN_OFFSETS, axis=-1)
    q_n, kc, vc, ks, vs, kw, vw, gates, q_r, k_r, v_r, g_r = parts
    scale = NSA_HEAD_DIM ** -0.5 * LOG2E
    gates_p = jnp.pad(gates, ((0, 0), (0, LANES - gates.shape[1])))
    w = jnp.concatenate([q_n * scale, kc, vc, ks, vs, kw, vw, gates_p, q_r, k_r, v_r, g_r], axis=1)
    assert w.shape[1] == _C_END
    return w.astype(BF16)


def _inproj(x, pre_g, w_all):
    B, T, D = x.shape
    tm = min(TM, T)
    nb = tm // LANES
    row = lambda w: pl.BlockSpec((1, tm, w), lambda b, i: (b, i, 0))
    out_shape = (
        jax.ShapeDtypeStruct((B, T // TQ, LANES, NSA_HEADS * TQ), BF16),
        jax.ShapeDtypeStruct((B, 2, T, KV_WIDTH), BF16),
        jax.ShapeDtypeStruct((B, T, KV_WIDTH), BF16),
        jax.ShapeDtypeStruct((B, T // LANES, VT_ROWS, LANES), BF16),
        jax.ShapeDtypeStruct((B, T, KV_WIDTH), BF16),
        jax.ShapeDtypeStruct((B, T // LANES, VT_ROWS, LANES), BF16),
        jax.ShapeDtypeStruct((B, T // TQ, GATE_ROWS, TQ), F32),
        jax.ShapeDtypeStruct((B, T, RET_WIDTH), BF16),
        jax.ShapeDtypeStruct((B, T, RET_WIDTH), BF16),
        jax.ShapeDtypeStruct((B, T, RET_WIDTH), BF16),
        jax.ShapeDtypeStruct((B, T, RET_WIDTH), F32),
    )
    vt_spec = pl.BlockSpec((1, nb, VT_ROWS, LANES), lambda b, i: (b, i, 0, 0))
    out_specs = (
        pl.BlockSpec((1, tm // TQ, LANES, NSA_HEADS * TQ), lambda b, i: (b, i, 0, 0)),
        pl.BlockSpec((1, 2, tm, KV_WIDTH), lambda b, i: (b, 0, i, 0)),
        row(KV_WIDTH), vt_spec, row(KV_WIDTH), vt_spec,
        pl.BlockSpec((1, tm // TQ, GATE_ROWS, TQ), lambda b, i: (b, i, 0, 0)),
        row(RET_WIDTH), row(RET_WIDTH), row(RET_WIDTH), row(RET_WIDTH),
    )
    return pl.pallas_call(
        _inproj_kernel,
        grid=(B, T // tm),
        in_specs=[pl.BlockSpec((1, tm, D), lambda b, i: (b, i, 0)),
                  _const_spec((1, D)), _const_spec(w_all.shape)],
        out_specs=out_specs,
        out_shape=out_shape,
        compiler_params=_cparams("parallel", "parallel"),
    )(x, pre_g.reshape(1, D), w_all)


def _compress_kernel(a_ref, pe_ref, w1_ref, w2_ref, rm_ref, t_ref, hi_scr):
    ncp = a_ref.shape[2]
    a = a_ref[0, 0].astype(F32)
    a_lo = (a + pe_ref[0, 0:1, :]).astype(BF16)
    a_hi = (a + pe_ref[0, 1:2, :]).astype(BF16)
    c = jnp.zeros((ncp, KV_WIDTH), F32)
    hi_scr[pl.ds(ncp, 8), :] = jnp.zeros((8, CMP_HIDDEN), F32)
    for g in range(NSA_KV_HEADS):
        p_lo = _mm(a_lo, w1_ref[0, 2 * g], preferred_element_type=F32)
        hi_scr[pl.ds(0, ncp), :] = _mm(a_hi, w1_ref[0, 2 * g + 1], preferred_element_type=F32)
        hdn = p_lo + hi_scr[pl.ds(1, ncp), :]
        hdn = (hdn * jax.nn.sigmoid(hdn)).astype(BF16)
        c = c + _mm(hdn, w2_ref[0, g], preferred_element_type=F32)
    rows = lax.broadcasted_iota(jnp.int32, c.shape, 0)
    c = jnp.where(rows < ncp - 1, c, 0.0)
    rm_ref[0, 0] = c.astype(BF16)
    for j in range(ncp // CH):
        t_ref[0, 0, j] = c[j * CH:(j + 1) * CH, :].T.astype(BF16)


def _compress_weights(cmp_pe, cmp_w1, cmp_w2):
    half = CMP_LEN // 2
    pe = cmp_pe.reshape(2, 2, half, 1, NSA_HEAD_DIM)
    pe = jnp.broadcast_to(pe, (2, 2, half, NSA_KV_HEADS, NSA_HEAD_DIM)).reshape(2, 2, half * KV_WIDTH)
    w1 = cmp_w1.reshape(2, 2, half, NSA_HEAD_DIM, CMP_HIDDEN)
    w1e = []
    for g in range(NSA_KV_HEADS):
        for hf in range(2):
            z = jnp.zeros((2, half, NSA_KV_HEADS, NSA_HEAD_DIM, CMP_HIDDEN), F32)
            z = z.at[:, :, g].set(w1[:, hf])
            w1e.append(z.reshape(2, half * KV_WIDTH, CMP_HIDDEN))
    w1e = jnp.stack(w1e, axis=1).astype(BF16)
    w2e = []
    for g in range(NSA_KV_HEADS):
        z = jnp.zeros((2, CMP_HIDDEN, NSA_KV_HEADS, NSA_HEAD_DIM), F32)
        z = z.at[:, :, g].set(cmp_w2)
        w2e.append(z.reshape(2, CMP_HIDDEN, KV_WIDTH))
    w2e = jnp.stack(w2e, axis=1).astype(BF16)
    return pe, w1e, w2e


def _compress(cmp, pe, w1e, w2e):
    B, _, T, _ = cmp.shape
    ncp = T // CMP_STRIDE
    a = cmp.reshape(B, 2, ncp, CMP_STRIDE * KV_WIDTH)
    kw = CMP_STRIDE * KV_WIDTH
    return pl.pallas_call(
        _compress_kernel,
        grid=(B, 2),
        in_specs=[pl.BlockSpec((1, 1, ncp, kw), lambda b, k: (b, k, 0, 0)),
                  pl.BlockSpec((1, 2, kw), lambda b, k: (k, 0, 0)),
                  pl.BlockSpec((1, 4, kw, CMP_HIDDEN), lambda b, k: (k, 0, 0, 0)),
                  pl.BlockSpec((1, 2, CMP_HIDDEN, KV_WIDTH), lambda b, k: (k, 0, 0, 0))],
        out_specs=(pl.BlockSpec((1, 1, ncp, KV_WIDTH), lambda b, k: (b, k, 0, 0)),
                   pl.BlockSpec((1, 1, ncp // CH, KV_WIDTH, CH), lambda b, k: (b, k, 0, 0, 0))),
        out_shape=(jax.ShapeDtypeStruct((B, 2, ncp, KV_WIDTH), BF16),
                   jax.ShapeDtypeStruct((B, 2, ncp // CH, KV_WIDTH, CH), BF16)),
        scratch_shapes=[pltpu.VMEM((ncp + 8, CMP_HIDDEN), F32)],
        compiler_params=_cparams("parallel", "parallel"),
    )(a, pe, w1e, w2e)


def _select_kernel(q_ref, kc_ref, vct_ref, ovl_ref, tab_ref, grp_ref, oc_ref, sel_ref, flag_ref,
                   s_scr, p_scr, imp_scr, oc_scr):
    tq = TQ
    nsel = ovl_ref.shape[1]
    q0 = pl.program_id(1) * tq
    qt = q_ref[0, 0]
    span = CH * CMP_STRIDE
    nch = (q0 + tq - CMP_LEN) // span + 1
    nfull = jnp.maximum((q0 - (span + CMP_LEN - CMP_STRIDE - 1)) // span + 1, 0)
    mrel = (lax.broadcasted_iota(jnp.int32, (CH, tq), 0) * CMP_STRIDE + (CMP_LEN - 1)
            - lax.broadcasted_iota(jnp.int32, (CH, tq), 1))
    heads = [slice(hh * tq, (hh + 1) * tq) for hh in range(NSA_HEADS)]

    def chunk_rows(c):
        return pl.ds(pl.multiple_of(c * CH, CH), CH)

    def scores(masked):
        def body(c, ms):
            rows = chunk_rows(c)
            s_c = _mm(kc_ref[0, 0, rows, :], qt, preferred_element_type=F32)
            ok = mrel <= q0 - c * span
            out = []
            for hh, sl in enumerate(heads):
                s = s_c[:, sl] + tab_ref[hh, rows, :]
                if masked:
                    s = jnp.where(ok, s, NEG)
                s_scr[rows, sl] = s
                out.append(jnp.maximum(ms[hh], jnp.max(s, axis=0, keepdims=True)))
            return tuple(out)
        return body

    m0 = tuple(jnp.full((1, tq), NEG, F32) for _ in heads)
    ms = lax.fori_loop(0, nfull, scores(False), m0)
    ms = lax.fori_loop(nfull, nch, scores(True), ms)

    def probs(masked):
        def body(c, dens):
            rows = chunk_rows(c)
            ok = mrel <= q0 - c * span
            out = []
            for hh, sl in enumerate(heads):
                p = jnp.exp2(s_scr[rows, sl] - ms[hh])
                if masked:
                    p = jnp.where(ok, p, 0.0)
                s_scr[rows, sl] = p
                out.append(dens[hh] + jnp.sum(p, axis=0, keepdims=True))
            return tuple(out)
        return body

    d0 = tuple(jnp.zeros((1, tq), F32) for _ in heads)
    dens = lax.fori_loop(0, nfull, probs(False), d0)
    dens = lax.fori_loop(nfull, nch, probs(True), dens)
    rden = [1.0 / jnp.maximum(d, 1e-30) for d in dens]

    imp_scr[...] = jnp.zeros_like(imp_scr)
    oc_scr[...] = jnp.zeros_like(oc_scr)

    def outputs(c, carry):
        rows = chunk_rows(c)
        his, los = [], []
        for g in range(NSA_KV_HEADS):
            psum = jnp.zeros((CH, tq), F32)
            for r in range(NSA_GROUP):
                hh = g * NSA_GROUP + r
                pn = s_scr[rows, heads[hh]] * rden[hh]
                psum = psum + pn
                p_scr[:, heads[hh]] = pn.astype(BF16)
            his.append(psum.astype(BF16))
            los.append((psum - his[-1].astype(F32)).astype(BF16))
        r = _mm(ovl_ref[c], jnp.concatenate(his + los, axis=1), preferred_element_type=F32)
        imp_scr[...] += r[:, :NSA_KV_HEADS * tq] + r[:, NSA_KV_HEADS * tq:]
        oc_scr[...] += _mm(vct_ref[0, 0, c], p_scr[...], preferred_element_type=F32)
        return carry

    lax.fori_loop(0, nch, outputs, 0)
    for hh, sl in enumerate(heads):
        oc_ref[0, 0, hh] = oc_scr[pl.ds((hh // NSA_GROUP) * NSA_HEAD_DIM, NSA_HEAD_DIM), sl]

    n_top = min(SLC_TOPN, nsel)
    n_seen = (q0 + tq - 1) // SLC_LEN + 1
    sizes = sorted({r for r in (nsel * i // 8 for i in range(1, 9)) if r >= n_top and r % 8 == 0})

    def select(rows):
        jb = lax.broadcasted_iota(jnp.int32, (rows, tq), 0)
        cur = (q0 + lax.broadcasted_iota(jnp.int32, (rows, tq), 1)) // SLC_LEN
        forced = (jb == 0) | (jb == cur) | (jb == cur - 1)
        cur1 = (q0 + lax.broadcasted_iota(jnp.int32, (1, tq), 1)) // SLC_LEN
        quota = n_top - (1 + (cur1 >= 1).astype(jnp.int32) + (cur1 >= 2).astype(jnp.int32))
        imps = tuple(jnp.where(forced, TAKEN, jnp.where(jb <= cur, imp_scr[pl.ds(0, rows), g * tq:(g + 1) * tq], -1.0))
                     for g in range(NSA_KV_HEADS))

        def pick(i, imps):
            out = []
            for imp in imps:
                best = jnp.max(imp, axis=0, keepdims=True)
                first = jnp.min(jnp.where(imp == best, jb, rows), axis=0, keepdims=True)
                first = jnp.where(i < quota, first, -1)
                out.append(jnp.where(jb == first, TAKEN, imp))
            return tuple(out)

        rounds = n_top - jnp.where(q0 >= 2 * SLC_LEN, 3, 1)
        imps = lax.fori_loop(0, rounds, pick, imps)
        ones = jnp.ones((ONES_ROWS, tq), BF16)
        cnt = jnp.zeros((ONES_ROWS, rows), F32)
        for g in range(NSA_KV_HEADS):
            taken = imps[g] < -2.0
            sel_ref[0, 0, g,pl.ds(0, rows), :] = jnp.where(taken, 0.0, NEG)
            if rows < nsel:
                sel_ref[0, 0, g,pl.ds(rows, nsel - rows), :] = jnp.full((nsel - rows, tq), NEG, F32)
            cnt = cnt + _mmg(ones, jnp.where(taken, 1.0, 0.0).astype(BF16), _NT,
                             preferred_element_type=F32)
        act = _mm(jnp.minimum(cnt, 1.0).astype(BF16), grp_ref[pl.ds(0, rows), :],
                  preferred_element_type=F32)
        flag_ref[0, 0] = (act[0:1, :] > 0.0).astype(jnp.int32)

    lo = 0
    for rows in sizes:
        cond = n_seen > lo
        if rows < nsel:
            cond = cond & (n_seen <= rows)
        pl.when(cond)(functools.partial(select, rows))
        lo = rows


def _overlap_t(T):
    ncp = T // CMP_STRIDE
    nc = (T - CMP_LEN) // CMP_STRIDE + 1
    nsel = T // SLC_LEN
    c_start = np.arange(nc) * CMP_STRIDE
    c_end = c_start + CMP_LEN - 1
    s_start = np.arange(nsel) * SLC_LEN
    s_end = s_start + SLC_LEN - 1
    ov = np.clip(np.minimum(c_end[:, None], s_end[None, :])
                 - np.maximum(c_start[:, None], s_start[None, :]) + 1, 0, None) / CMP_LEN
    out = np.zeros((nsel, ncp), np.float32)
    out[:, :nc] = ov.T
    out = out.reshape(nsel, ncp // CH, CH).transpose(1, 0, 2)
    return jnp.asarray(out, dtype=BF16)


def _select_tables(T):
    ncp = T // CMP_STRIDE
    nsel = T // SLC_LEN
    nkt = T // min(TK, T)
    n = np.arange(ncp, dtype=np.float64)[:, None]
    l = np.arange(TQ, dtype=np.float64)[None, :]
    dist = l - (n * CMP_STRIDE + (CMP_LEN - 1) / 2.0)
    tab = np.stack([-a * dist for a in ALIBI2]).astype(np.float32)
    grp = (np.arange(nsel)[:, None] // (nsel // nkt) == np.arange(nkt)[None, :]).astype(np.float32)
    grp[0, :] = 0.0
    return jnp.asarray(tab), jnp.asarray(grp, dtype=BF16)


def _select(q_hm, c_rm, c_t):
    B, T = q_hm.shape[0], q_hm.shape[1] * TQ
    ncp = T // CMP_STRIDE
    nsel = T // SLC_LEN
    nqt = T // TQ
    nkt = T // min(TK, T)
    tab, grp = _select_tables(T)
    return pl.pallas_call(
        _select_kernel,
        grid=(B, nqt),
        in_specs=[pl.BlockSpec((1, 1, LANES, NSA_HEADS * TQ), lambda b, i: (b, i, 0, 0)),
                  pl.BlockSpec((1, 1, ncp, KV_WIDTH), lambda b, i: (b, 0, 0, 0)),
                  pl.BlockSpec((1, 1, ncp // CH, KV_WIDTH, CH), lambda b, i: (b, 1, 0, 0, 0)),
                  _const_spec((ncp // CH, nsel, CH)), _const_spec(tab.shape), _const_spec(grp.shape)],
        out_specs=(pl.BlockSpec((1, 1, NSA_HEADS, NSA_HEAD_DIM, TQ), lambda b, i: (b, i, 0, 0, 0)),
                   pl.BlockSpec((1, 1, NSA_KV_HEADS, nsel, TQ), lambda b, i: (b, i, 0, 0, 0)),
                   pl.BlockSpec((1, 1, 1, nkt), lambda b, i: (b, i, 0, 0))),
        out_shape=(jax.ShapeDtypeStruct((B, nqt, NSA_HEADS, NSA_HEAD_DIM, TQ), F32),
                   jax.ShapeDtypeStruct((B, nqt, NSA_KV_HEADS, nsel, TQ), F32),
                   jax.ShapeDtypeStruct((B, nqt, 1, nkt), jnp.int32)),
        scratch_shapes=[pltpu.VMEM((ncp, NSA_HEADS * TQ), F32),
                        pltpu.VMEM((CH, NSA_HEADS * TQ), BF16),
                        pltpu.VMEM((nsel, NSA_KV_HEADS * TQ), F32),
                        pltpu.VMEM((KV_WIDTH, NSA_HEADS * TQ), F32)],
        compiler_params=_cparams("parallel", "arbitrary"),
    )(q_hm, c_rm, c_t, _overlap_t(T), tab, grp)


def _attn_kernel(flags_ref, q_ref, ks_ref, vst_ref, kw_ref, vwt_ref, sel_ref, oc_ref, gt_ref,
                 gain_ref, kf_ref, kh_ref, qf_ref, wb_ref, o_ref, acc_scr, p_scr, m_scr, alpha_scr,
                 pw_scr, ow_scr, qf_scr, act_scr):
    tq = TQ
    T = ks_ref.shape[1]
    qi = pl.program_id(1)
    q0 = qi * tq
    tk = kf_ref.shape[0]
    wk = wb_ref.shape[1]
    nkt = T // tk
    bpt = tk // SLC_LEN
    vpt = tk // LANES
    assert bpt == 8
    qt = q_ref[0, 0]
    heads = [slice(hh * tq, (hh + 1) * tq) for hh in range(NSA_HEADS)]

    qf_scr[pl.ds(0, LANES), :] = qt
    qf_scr[pl.ds(LANES, LANES), :] = qf_ref[...]
    row8 = lax.broadcasted_iota(jnp.int32, (8, tq), 0)

    def digits3(v):
        d1 = v.astype(BF16).astype(F32)
        d2 = (v - d1).astype(BF16).astype(F32)
        return d1, d2, v - d1 - d2

    def tile(kt, rows, first, head=False):
        k0 = kt * tk
        shift = (q0 - k0).astype(F32)
        for g in range(NSA_KV_HEADS):
            sb = sel_ref[0, 0, g,pl.ds(pl.multiple_of(kt * bpt, bpt), bpt), :]
            for r in range(NSA_GROUP):
                hh = g * NSA_GROUP + r
                d = digits3(jnp.full((8, tq), -ALIBI2[hh], F32) * shift)
                low = jnp.where(row8 == 0, d[0], jnp.where(row8 == 1, d[1], jnp.where(row8 == 2, d[2], 0.0)))
                if head:
                    e = digits3(jnp.full((8, tq), -ALIBI2[hh], F32) * q0.astype(F32))
                    low = jnp.where(row8 == 3, e[0], jnp.where(row8 == 4, e[1], jnp.where(row8 == 5, e[2], low)))
                qf_scr[pl.ds(LANES, 16), heads[hh]] = jnp.concatenate([sb, low], axis=0).astype(BF16)
                if head:
                    qf_scr[pl.ds(LANES + 16, 16), heads[hh]] = jnp.concatenate(
                        [sel_ref[0, 0, g,pl.ds(0, 8), :], jnp.zeros((8, tq), F32)], axis=0).astype(BF16)
        lhs = jnp.concatenate([ks_ref[0, pl.ds(pl.multiple_of(k0, tk), rows), :],
                               kf_ref[pl.ds(0, rows), :]], axis=1)
        vts = [vst_ref[0, kt * vpt + j] for j in range(rows // LANES)]
        if head:
            lhs = jnp.concatenate(
                [jnp.concatenate([ks_ref[0, pl.ds(0, LANES), :], kh_ref[...]], axis=1), lhs], axis=0)
            vts = [vst_ref[0, 0]] + vts
            rows = rows + LANES
        vt = jnp.concatenate(vts, axis=1)
        if first:
            c_i = lax.broadcasted_iota(jnp.int32, (rows, tq), 0)
            l_i = lax.broadcasted_iota(jnp.int32, (rows, tq), 1)
            if head:
                causal = (c_i < LANES) | (c_i + (k0 - q0 - LANES) <= l_i)
            else:
                causal = c_i + (k0 - q0) <= l_i
        s_all = _mm(lhs, qf_scr[...], preferred_element_type=F32)
        for hh, sl in enumerate(heads):
            s = s_all[:, sl]
            if first:
                s = jnp.where(causal, s, NEG)
                m_new = jnp.max(s, axis=0, keepdims=True)
            else:
                m_old = m_scr[:, sl]
                m_new = jnp.maximum(m_old, jnp.max(s, axis=0, keepdims=True))
                alpha_scr[:, sl] = jnp.exp2(m_old - m_new)
            m_scr[:, sl] = m_new
            p_scr[pl.ds(0, rows), sl] = jnp.exp2(s - m_new).astype(BF16)
        pv = _mm(vt, p_scr[pl.ds(0, rows), :], preferred_element_type=F32)
        if first:
            acc_scr[...] = pv
        else:
            acc_scr[...] = acc_scr[...] * alpha_scr[...] + pv

    kt_diag = q0 // tk
    fbase = (pl.program_id(0) * pl.num_programs(1) + qi) * nkt
    with_head = (kt_diag > 0) & (flags_ref[fbase] == 0)
    for jj in range(tk // tq):
        here = qi % (tk // tq) == jj
        pl.when(here & with_head)(functools.partial(tile, kt_diag, (jj + 1) * tq, True, True))
        pl.when(here & jnp.logical_not(with_head))(
            functools.partial(tile, kt_diag, (jj + 1) * tq, True, False))

    def scan(kt, n):
        act_scr[n] = kt
        return n + (flags_ref[fbase + kt] != 0).astype(jnp.int32)

    n_act = lax.fori_loop(0, kt_diag, scan, 0)

    def body(i, carry):
        tile(act_scr[i], tk, False)
        return carry

    lax.fori_loop(0, n_act, body, 0)

    def window(ws, general):
        sw_all = _mm(kw_ref[0, pl.ds(pl.multiple_of(ws, LANES), wk), :], qt,
                     preferred_element_type=F32)
        vwt = jnp.concatenate([vwt_ref[0, ws // LANES + j] for j in range(wk // LANES)], axis=1)
        if general:
            kpos = ws + lax.broadcasted_iota(jnp.int32, (wk, tq), 0)
            dist = q0 + lax.broadcasted_iota(jnp.int32, (wk, tq), 1) - kpos
            wmask = (dist >= 0) & (dist < WINDOW)
            distf = dist.astype(F32)
        for hh, sl in enumerate(heads):
            if general:
                s = jnp.where(wmask, sw_all[:, sl] - ALIBI2[hh] * distf, NEG)
            else:
                s = sw_all[:, sl] + wb_ref[hh]
            pw_scr[:, sl] = jnp.exp2(s - jnp.max(s, axis=0, keepdims=True)).astype(BF16)
        ow_scr[...] = _mm(vwt, pw_scr[...], preferred_element_type=F32)

    @pl.when(q0 >= WINDOW)
    def _():
        window(q0 - WINDOW, False)

    @pl.when(q0 < WINDOW)
    def _():
        window(q0 * 0, True)

    gs = jax.nn.sigmoid(gt_ref[0, 0])
    outs = []
    ssq = jnp.zeros((1, tq), F32)
    for hh in range(NSA_HEADS):
        sl = heads[hh]
        ch = pl.ds((hh // NSA_GROUP) * NSA_HEAD_DIM, NSA_HEAD_DIM)
        o_s = acc_scr[ch, sl] / jnp.maximum(acc_scr[KV_WIDTH:KV_WIDTH + 1, sl], 1e-30)
        o_w = ow_scr[ch, sl] / jnp.maximum(ow_scr[KV_WIDTH:KV_WIDTH + 1, sl], 1e-30)
        o = (gs[3 * hh:3 * hh + 1, :] * oc_ref[0, 0, hh] + gs[3 * hh + 1:3 * hh + 2, :] * o_s
             + gs[3 * hh + 2:3 * hh + 3, :] * o_w)
        ssq = ssq + jnp.sum(o * o, axis=0, keepdims=True)
        outs.append(o)
    rinv = lax.rsqrt(ssq / NSA_WIDTH + EPS)
    for j in range(NSA_HEADS // 2):
        pair = jnp.concatenate([outs[2 * j] * rinv * gain_ref[2 * j],
                                outs[2 * j + 1] * rinv * gain_ref[2 * j + 1]], axis=0)
        o_ref[0, :, j * LANES:(j + 1) * LANES] = pair.T.astype(BF16)


def _attn_gain(nsa_out_g):
    g = nsa_out_g.reshape(NSA_HEADS, NSA_HEAD_DIM)
    return jnp.broadcast_to(g[:, :, None], (NSA_HEADS, NSA_HEAD_DIM, TQ))


def _bf16_digits(x, n):
    out = []
    for _ in range(n):
        d = float(np.asarray(x, np.float32).astype(jnp.bfloat16).astype(np.float32))
        out.append(d)
        x = x - d
    return out


def _attn_tables(T):
    tk = min(TK, T)
    wk = min(WKEYS, T)
    c = np.arange(tk)
    kf = np.zeros((tk, LANES), np.float32)
    kf[c, c // SLC_LEN] = 1.0
    kf[:, 8:11] = 1.0
    kf[:, 32:35] = (c // 32)[:, None]
    kf[:, 35:38] = (c % 32)[:, None]
    ch = np.arange(LANES)
    kh = np.zeros((LANES, LANES), np.float32)
    kh[ch, 16 + ch // SLC_LEN] = 1.0
    kh[:, 11:14] = 1.0
    kh[:, 32:35] = (ch // 32)[:, None]
    kh[:, 35:38] = (ch % 32)[:, None]
    qf = np.zeros((LANES, NSA_HEADS * TQ), np.float32)
    for hh, a in enumerate(ALIBI2):
        d = _bf16_digits(a, 3)
        for i in range(3):
            qf[32 + i, hh * TQ:(hh + 1) * TQ] = 32.0 * d[i]
            qf[35 + i, hh * TQ:(hh + 1) * TQ] = d[i]
    l = np.arange(TQ, dtype=np.float64)[None, :]
    dist = WINDOW + l - np.arange(wk, dtype=np.float64)[:, None]
    inside = (dist >= 0) & (dist < WINDOW)
    wb = np.stack([np.where(inside, -a * dist, NEG) for a in ALIBI2]).astype(np.float32)
    return (jnp.asarray(kf, dtype=BF16), jnp.asarray(kh, dtype=BF16), jnp.asarray(qf, dtype=BF16),
            jnp.asarray(wb))


def _attn(q_hm, ks, vst, kw, vwt, sel, flags, oc, gt, gain_b):
    B, T = q_hm.shape[0], q_hm.shape[1] * TQ
    nsel = T // SLC_LEN
    nqt = T // TQ
    kf, kh, qf, wb = _attn_tables(T)
    tk, wk = kf.shape[0], wb.shape[1]
    once = dict(pipeline_mode=pl.Buffered(1))
    full_k = pl.BlockSpec((1, T, KV_WIDTH), lambda b, i, f: (b, 0, 0), **once)
    full_vt = pl.BlockSpec((1, T // LANES, VT_ROWS, LANES), lambda b, i, f: (b, 0, 0, 0), **once)
    const = lambda shape: pl.BlockSpec(shape, lambda b, i, f: (0,) * len(shape), **once)
    grid_spec = pltpu.PrefetchScalarGridSpec(
        num_scalar_prefetch=1,
        grid=(B, nqt),
        in_specs=[pl.BlockSpec((1, 1, LANES, NSA_HEADS * TQ), lambda b, i, f: (b, i, 0, 0)),
                  full_k, full_vt, full_k, full_vt,
                  pl.BlockSpec((1, 1, NSA_KV_HEADS, nsel, TQ), lambda b, i, f: (b, i, 0, 0, 0)),
                  pl.BlockSpec((1, 1, NSA_HEADS, NSA_HEAD_DIM, TQ), lambda b, i, f: (b, i, 0, 0, 0)),
                  pl.BlockSpec((1, 1, GATE_ROWS, TQ), lambda b, i, f: (b, i, 0, 0)),
                  const((NSA_HEADS, NSA_HEAD_DIM, TQ)), const(kf.shape), const(kh.shape),
                  const(qf.shape), const(wb.shape)],
        out_specs=pl.BlockSpec((1, TQ, NSA_WIDTH), lambda b, i, f: (b, i, 0)),
        scratch_shapes=[pltpu.VMEM((VT_ROWS, NSA_HEADS * TQ), F32),
                        pltpu.VMEM((tk + LANES, NSA_HEADS * TQ), BF16),
                        pltpu.VMEM((1, NSA_HEADS * TQ), F32),
                        pltpu.VMEM((1, NSA_HEADS * TQ), F32),
                        pltpu.VMEM((wk, NSA_HEADS * TQ), BF16),
                        pltpu.VMEM((VT_ROWS, NSA_HEADS * TQ), F32),
                        pltpu.VMEM((2 * LANES, NSA_HEADS * TQ), BF16),
                        pltpu.SMEM((T // tk,), jnp.int32)])
    return pl.pallas_call(
        _attn_kernel,
        grid_spec=grid_spec,
        out_shape=jax.ShapeDtypeStruct((B, T, NSA_WIDTH), BF16),
        compiler_params=_cparams("parallel", "arbitrary"),
    )(flags.reshape(-1), q_hm, ks, vst, kw, vwt, sel, oc, gt, gain_b, kf, kh, qf, wb)


def _retention_kernel(q_ref, k_ref, v_ref, g_ref, dm_ref, qd_ref, kd_ref, gn_ref, o_ref, st_scr,
                      *, chunk_dec):
    rt = q_ref.shape[1]
    C, d = RET_CHUNK, RET_HEAD_DIM

    @pl.when(pl.program_id(1) == 0)
    def _():
        st_scr[...] = jnp.zeros_like(st_scr)

    for c in range(rt // C):
        rows = slice(c * C, (c + 1) * C)
        for h in range(RET_HEADS):
            cols = slice(h * d, (h + 1) * d)
            q = q_ref[0, rows, cols]
            k = k_ref[0, rows, cols]
            v = v_ref[0, rows, cols]
            state = st_scr[h]
            s = _mmg(q, k, _NT, preferred_element_type=F32) * dm_ref[h]
            inner = _mm(s.astype(BF16), v, preferred_element_type=F32)
            cross = _mm(q, state.astype(BF16), preferred_element_type=F32) * qd_ref[h]
            kd = (k.astype(F32) * kd_ref[h]).astype(BF16)
            st_scr[h] = state * chunk_dec[h] + _mmg(kd, v, _TN, preferred_element_type=F32)
            o = inner + cross
            mu = jnp.mean(o, axis=-1, keepdims=True)
            oc = o - mu
            var = jnp.mean(oc * oc, axis=-1, keepdims=True)
            o = oc * lax.rsqrt(var + EPS) * gn_ref[:, cols]
            gg = g_ref[0, rows, cols]
            o_ref[0, rows, cols] = (o * (gg * jax.nn.sigmoid(gg))).astype(BF16)


def _retention_consts():
    C, d = RET_CHUNK, RET_HEAD_DIM
    log_g = np.log(1.0 - np.exp2(-5.0 - np.arange(RET_HEADS, dtype=np.float64)))
    pos = np.arange(C, dtype=np.float64)
    diff = pos[:, None] - pos[None, :]
    scale = d ** -0.5
    dmask = np.where(diff >= 0, np.exp(np.maximum(diff, 0.0) * log_g[:, None, None]), 0.0) * scale
    q_dec = np.exp((pos + 1.0) * log_g[:, None])
    k_dec = np.exp((C - 1.0 - pos) * log_g[:, None]) * scale
    chunk_dec = tuple(float(x) for x in np.exp(C * log_g))
    bc = lambda a: jnp.asarray(np.broadcast_to(a[:, :, None], (RET_HEADS, C, d)).astype(np.float32))
    return jnp.asarray(dmask.astype(np.float32)), bc(q_dec), bc(k_dec), chunk_dec


def _retention(qr, kr, vr, gr, gn_gain):
    B, T, W = qr.shape
    rt = min(2 * TM, T)
    dmask, qd, kd, chunk_dec = _retention_consts()
    row = pl.BlockSpec((1, rt, W), lambda b, i: (b, i, 0))
    cst = _const_spec((RET_HEADS, RET_CHUNK, RET_HEAD_DIM))
    return pl.pallas_call(
        functools.partial(_retention_kernel, chunk_dec=chunk_dec),
        grid=(B, T // rt),
        in_specs=[row, row, row, row, _const_spec(dmask.shape), cst, cst, _const_spec((1, W))],
        out_specs=row,
        out_shape=jax.ShapeDtypeStruct((B, T, W), BF16),
        scratch_shapes=[pltpu.VMEM((RET_HEADS, RET_HEAD_DIM, RET_HEAD_DIM), F32)],
        compiler_params=_cparams("parallel", "arbitrary"),
    )(qr, kr, vr, gr, dmask, qd, kd, gn_gain.reshape(1, W))


def _outproj_kernel(x_ref, on_ref, or_ref, wn_ref, wr_ref, post_ref, o_ref):
    y = (_mm(on_ref[0], wn_ref[...], preferred_element_type=F32)
         + _mm(or_ref[0], wr_ref[...], preferred_element_type=F32))
    o_ref[0] = x_ref[0] + _rms(y, post_ref[...])


def _outproj_weights(w_out):
    return w_out[:NSA_WIDTH].astype(BF16), w_out[NSA_WIDTH:].astype(BF16)


def _outproj(x, o_nsa, o_ret, wn, wr, post_g):
    B, T, D = x.shape
    tm = min(2 * TM, T)
    row = lambda w: pl.BlockSpec((1, tm, w), lambda b, i: (b, i, 0))
    return pl.pallas_call(
        _outproj_kernel,
        grid=(B, T // tm),
        in_specs=[row(D), row(NSA_WIDTH), row(RET_WIDTH), _const_spec(wn.shape), _const_spec(wr.shape),
                  _const_spec((1, D))],
        out_specs=row(D),
        out_shape=jax.ShapeDtypeStruct(x.shape, F32),
        compiler_params=_cparams("parallel", "parallel"),
    )(x, o_nsa, o_ret, wn, wr, post_g.reshape(1, D))


def _memkv_kernel(m_ref, g_ref, wk_ref, wv_ref, k_ref, v_ref):
    h = _rms(m_ref[0], g_ref[...]).astype(BF16)
    k_ref[0] = _mm(h, wk_ref[...], preferred_element_type=F32).astype(BF16)
    v_ref[0] = _mm(h, wv_ref[...], preferred_element_type=F32).astype(BF16)


def _memkv(mem, g, wk, wv):
    B, M, D = mem.shape
    blk = pl.BlockSpec((1, M, D), lambda b: (b, 0, 0))
    return pl.pallas_call(
        _memkv_kernel,
        grid=(B,),
        in_specs=[blk, _const_spec((1, D)), _const_spec(wk.shape), _const_spec(wv.shape)],
        out_specs=(blk, blk),
        out_shape=(jax.ShapeDtypeStruct(mem.shape, BF16), jax.ShapeDtypeStruct(mem.shape, BF16)),
        compiler_params=_cparams("parallel"),
    )(mem, g.reshape(1, D), wk, wv)


def _xattn_kernel(x_ref, pre_ref, wq_ref, k_ref, v_ref, wo_ref, post_ref, o_ref):
    x = x_ref[0]
    h = _rms(x, pre_ref[...]).astype(BF16)
    q = (_mm(h, wq_ref[...], preferred_element_type=F32) * (MEM_HEAD_DIM ** -0.5)).astype(BF16)
    heads = []
    for hd in range(MEM_HEADS):
        cols = slice(hd * MEM_HEAD_DIM, (hd + 1) * MEM_HEAD_DIM)
        s = _mmg(q[:, cols], k_ref[0, :, cols], _NT, preferred_element_type=F32)
        p = jnp.exp(s - jnp.max(s, axis=-1, keepdims=True))
        den = jnp.sum(p, axis=-1, keepdims=True)
        oh = _mm(p.astype(BF16), v_ref[0, :, cols], preferred_element_type=F32) / den
        heads.append(oh.astype(BF16))
    o = jnp.concatenate(heads, axis=1)
    y = _mm(o, wo_ref[...], preferred_element_type=F32)
    o_ref[0] = x + _rms(y, post_ref[...])


def _xattn(x, pre_g, wq, km, vm, wo, post_g):
    B, T, D = x.shape
    M = km.shape[1]
    tm = min(2 * TM, T)
    row = pl.BlockSpec((1, tm, D), lambda b, i: (b, i, 0))
    kv = pl.BlockSpec((1, M, D), lambda b, i: (b, 0, 0))
    return pl.pallas_call(
        _xattn_kernel,
        grid=(B, T // tm),
        in_specs=[row, _const_spec((1, D)), _const_spec(wq.shape), kv, kv, _const_spec(wo.shape),
                  _const_spec((1, D))],
        out_specs=row,
        out_shape=jax.ShapeDtypeStruct(x.shape, F32),
        compiler_params=_cparams("parallel", "parallel"),
    )(x, pre_g.reshape(1, D), wq, km, vm, wo, post_g.reshape(1, D))


def _hybrid_mixer(x, pre_g, w_in, cmp_pe, cmp_w1, cmp_w2, nsa_out_g, ret_gn_g, w_out, post_g):
    q_hm, cmp, ks, vst, kw, vwt, gt, qr, kr, vr, gr = _inproj(x, pre_g, _inproj_weight(w_in))
    c_rm, c_t = _compress(cmp, *_compress_weights(cmp_pe, cmp_w1, cmp_w2))
    oc, sel, flags = _select(q_hm, c_rm, c_t)
    o_nsa = _attn(q_hm, ks, vst, kw, vwt, sel, flags, oc, gt, _attn_gain(nsa_out_g))
    o_ret = _retention(qr, kr, vr, gr, ret_gn_g)
    wn, wr = _outproj_weights(w_out)
    return _outproj(x, o_nsa, o_ret, wn, wr, post_g)


def kernel(x, mem, ffn1_pre_g, ffn1_w_gate, ffn1_w_up, ffn1_w_down, ffn1_post_g, mix_pre_g, w_in, cmp_pe, cmp_w1, cmp_w2, nsa_out_g, ret_gn_g, w_out, mix_post_g, xa_pre_g, xa_mem_g, xa_wq, xa_wk, xa_wv, xa_wo, xa_post_g, ffn2_pre_g, ffn2_w_gate, ffn2_w_up, ffn2_w_down, ffn2_post_g):
    depth = w_in.shape[0]
    bf = lambda w: w.astype(BF16)
    for l in range(depth):
        x = _ffn(x, ffn1_pre_g[l], bf(ffn1_w_gate[l]), bf(ffn1_w_up[l]), bf(ffn1_w_down[l]),
                 ffn1_post_g[l])
        x = _hybrid_mixer(x, mix_pre_g[l], w_in[l], cmp_pe[l], cmp_w1[l], cmp_w2[l], nsa_out_g[l],
                          ret_gn_g[l], w_out[l], mix_post_g[l])
        km, vm = _memkv(mem, xa_mem_g[l], bf(xa_wk[l]), bf(xa_wv[l]))
        x = _xattn(x, xa_pre_g[l], bf(xa_wq[l]), km, vm, bf(xa_wo[l]), xa_post_g[l])
        x = _ffn(x, ffn2_pre_g[l], bf(ffn2_w_gate[l]), bf(ffn2_w_up[l]), bf(ffn2_w_down[l]),
                 ffn2_post_g[l])
    return x
```

```python
import functools

import numpy as np
import jax
import jax.numpy as jnp
from jax import lax
from jax.experimental import pallas as pl
from jax.experimental.pallas import tpu as pltpu

F32 = jnp.float32
BF16 = jnp.bfloat16

D_MODEL = 1024
NSA_HEADS = 8
NSA_HEAD_DIM = 64
NSA_KV_HEADS = 2
NSA_GROUP = NSA_HEADS // NSA_KV_HEADS
NSA_WIDTH = NSA_HEADS * NSA_HEAD_DIM
KV_WIDTH = NSA_KV_HEADS * NSA_HEAD_DIM
CMP_LEN = 32
CMP_STRIDE = 16
CMP_HIDDEN = 256
SLC_LEN = 64
SLC_TOPN = 16
WINDOW = 512
RET_HEADS = 4
RET_HEAD_DIM = 128
RET_WIDTH = RET_HEADS * RET_HEAD_DIM
RET_CHUNK = 256
MEM_HEADS = 4
MEM_HEAD_DIM = D_MODEL // MEM_HEADS
D_FF = 2816
EPS = 1e-6
NEG = -1e30
TAKEN = -3e38
GATE_ROWS = 32

IN_SIZES = (NSA_WIDTH, KV_WIDTH, KV_WIDTH, KV_WIDTH, KV_WIDTH, KV_WIDTH, KV_WIDTH,
            NSA_HEADS * 3, RET_WIDTH, RET_WIDTH, RET_WIDTH, RET_WIDTH)
IN_OFFSETS = tuple(int(o) for o in np.cumsum(IN_SIZES)[:-1])

LANES = 128
LOG2E = 1.4426950408889634
ALIBI2 = tuple(float(2.0 ** (-8.0 * i / NSA_HEADS)) * LOG2E for i in range(1, NSA_HEADS + 1))
ONES_ROWS = 16
VT_ROWS = KV_WIDTH + ONES_ROWS
CH = 256

TM = 512
TQ = 128
TK = 512
WKEYS = WINDOW + TQ
VMEM_LIMIT = 56 * 1024 * 1024

_NT = (((1,), (1,)), ((), ()))
_TN = (((0,), (0,)), ((), ()))


def _mm(a, b, preferred_element_type=F32):
    return jnp.dot(a, b, preferred_element_type=preferred_element_type)


def _mmg(a, b, dims, preferred_element_type=F32):
    return lax.dot_general(a, b, dims, preferred_element_type=preferred_element_type)


def _cparams(*sem):
    return pltpu.CompilerParams(dimension_semantics=sem, vmem_limit_bytes=VMEM_LIMIT)


def _rms(x, g):
    return x * lax.rsqrt(jnp.mean(x * x, axis=-1, keepdims=True) + EPS) * g


def _const_spec(shape):
    nd = len(shape)
    return pl.BlockSpec(shape, lambda *_: (0,) * nd, pipeline_mode=pl.Buffered(1))


def _ffn_kernel(x_ref, pre_ref, wg_ref, wu_ref, wd_ref, post_ref, o_ref):
    half = x_ref.shape[1] // 2
    for r in range(2):
        rows = slice(r * half, (r + 1) * half)
        x = x_ref[0, rows, :]
        h = _rms(x, pre_ref[...]).astype(BF16)
        g = _mm(h, wg_ref[...], preferred_element_type=F32)
        u = _mm(h, wu_ref[...], preferred_element_type=F32)
        a = (g * jax.nn.sigmoid(g) * u).astype(BF16)
        y = _mm(a, wd_ref[...], preferred_element_type=F32)
        o_ref[0, rows, :] = x + 0.5 * _rms(y, post_ref[...])


def _ffn(x, pre_g, wg, wu, wd, post_g):
    B, T, D = x.shape
    tm = min(2 * TM, T)
    return pl.pallas_call(
        _ffn_kernel,
        grid=(B, T // tm),
        in_specs=[pl.BlockSpec((1, tm, D), lambda b, i: (b, i, 0)),
                  _const_spec((1, D)), _const_spec(wg.shape), _const_spec(wu.shape),
                  _const_spec(wd.shape), _const_spec((1, D))],
        out_specs=pl.BlockSpec((1, tm, D), lambda b, i: (b, i, 0)),
        out_shape=jax.ShapeDtypeStruct(x.shape, F32),
        compiler_params=_cparams("parallel", "parallel"),
    )(x, pre_g.reshape(1, D), wg, wu, wd, post_g.reshape(1, D))


_C_Q = 0
_C_CMP = _C_Q + NSA_WIDTH
_C_KS = _C_CMP + 2 * KV_WIDTH
_C_VS = _C_KS + KV_WIDTH
_C_KW = _C_VS + KV_WIDTH
_C_VW = _C_KW + KV_WIDTH
_C_GATE = _C_VW + KV_WIDTH
_C_QR = _C_GATE + LANES
_C_KR = _C_QR + RET_WIDTH
_C_VR = _C_KR + RET_WIDTH
_C_GR = _C_VR + RET_WIDTH
_C_END = _C_GR + RET_WIDTH


def _inproj_kernel(x_ref, pre_ref, w_ref, q_ref, cmp_ref, ks_ref, vst_ref, kw_ref, vwt_ref,
                   gt_ref, qr_ref, kr_ref, vr_ref, gr_ref):
    tm = x_ref.shape[1]
    h = _rms(x_ref[0], pre_ref[...]).astype(BF16)

    def proj(lo, hi):
        return _mm(h, w_ref[:, lo:hi], preferred_element_type=F32)

    q = proj(_C_Q, _C_CMP)
    zero = jnp.zeros((NSA_HEAD_DIM, TQ), F32)
    for j in range(tm // TQ):
        for pair in range(NSA_HEADS // 2):
            t = q[j * TQ:(j + 1) * TQ, pair * LANES:(pair + 1) * LANES].T
            for k in range(2):
                hh = 2 * pair + k
                dims = t[k * NSA_HEAD_DIM:(k + 1) * NSA_HEAD_DIM, :]
                slab = [dims, zero] if hh // NSA_GROUP == 0 else [zero, dims]
                q_ref[0, j, :, hh * TQ:(hh + 1) * TQ] = jnp.concatenate(slab, axis=0).astype(BF16)
    c = proj(_C_CMP, _C_KS)
    cmp_ref[0, 0] = c[:, :KV_WIDTH].astype(BF16)
    cmp_ref[0, 1] = c[:, KV_WIDTH:].astype(BF16)
    ks_ref[0] = proj(_C_KS, _C_VS).astype(BF16)
    kw_ref[0] = proj(_C_KW, _C_VW).astype(BF16)
    vs = proj(_C_VS, _C_KW)
    vw = proj(_C_VW, _C_GATE)
    ones = jnp.ones((ONES_ROWS, LANES), BF16)
    for j in range(tm // LANES):
        vst_ref[0, j, :KV_WIDTH, :] = vs[j * LANES:(j + 1) * LANES, :].T.astype(BF16)
        vwt_ref[0, j, :KV_WIDTH, :] = vw[j * LANES:(j + 1) * LANES, :].T.astype(BF16)
        vst_ref[0, j, KV_WIDTH:, :] = ones
        vwt_ref[0, j, KV_WIDTH:, :] = ones
    gates_t = proj(_C_GATE, _C_QR).T
    for j in range(tm // TQ):
        gt_ref[0, j] = gates_t[:GATE_ROWS, j * TQ:(j + 1) * TQ]
    qr_ref[0] = proj(_C_QR, _C_KR).astype(BF16)
    kr_ref[0] = proj(_C_KR, _C_VR).astype(BF16)
    vr_ref[0] = proj(_C_VR, _C_GR).astype(BF16)
    gr_ref[0] = proj(_C_GR, _C_END)


def _inproj_weight(w_in):
    parts = jnp.split(w_in, IN_OFFSETS, axis=-1)
    q_n, kc, vc, ks, vs, kw, vw, gates, q_r, k_r, v_r, g_r = parts
    scale = NSA_HEAD_DIM ** -0.5 * LOG2E
    gates_p = jnp.pad(gates, ((0, 0), (0, LANES - gates.shape[1])))
    w = jnp.concatenate([q_n * scale, kc, vc, ks, vs, kw, vw, gates_p, q_r, k_r, v_r, g_r], axis=1)
    assert w.shape[1] == _C_END
    return w.astype(BF16)


def _inproj(x, pre_g, w_all):
    B, T, D = x.shape
    tm = min(TM, T)
    nb = tm // LANES
    row = lambda w: pl.BlockSpec((1, tm, w), lambda b, i: (b, i, 0))
    out_shape = (
        jax.ShapeDtypeStruct((B, T // TQ, LANES, NSA_HEADS * TQ), BF16),
        jax.ShapeDtypeStruct((B, 2, T, KV_WIDTH), BF16),
        jax.ShapeDtypeStruct((B, T, KV_WIDTH), BF16),
        jax.ShapeDtypeStruct((B, T // LANES, VT_ROWS, LANES), BF16),
        jax.ShapeDtypeStruct((B, T, KV_WIDTH), BF16),
        jax.ShapeDtypeStruct((B, T // LANES, VT_ROWS, LANES), BF16),
        jax.ShapeDtypeStruct((B, T // TQ, GATE_ROWS, TQ), F32),
        jax.ShapeDtypeStruct((B, T, RET_WIDTH), BF16),
        jax.ShapeDtypeStruct((B, T, RET_WIDTH), BF16),
        jax.ShapeDtypeStruct((B, T, RET_WIDTH), BF16),
        jax.ShapeDtypeStruct((B, T, RET_WIDTH), F32),
    )
    vt_spec = pl.BlockSpec((1, nb, VT_ROWS, LANES), lambda b, i: (b, i, 0, 0))
    out_specs = (
        pl.BlockSpec((1, tm // TQ, LANES, NSA_HEADS * TQ), lambda b, i: (b, i, 0, 0)),
        pl.BlockSpec((1, 2, tm, KV_WIDTH), lambda b, i: (b, 0, i, 0)),
        row(KV_WIDTH), vt_spec, row(KV_WIDTH), vt_spec,
        pl.BlockSpec((1, tm // TQ, GATE_ROWS, TQ), lambda b, i: (b, i, 0, 0)),
        row(RET_WIDTH), row(RET_WIDTH), row(RET_WIDTH), row(RET_WIDTH),
    )
    return pl.pallas_call(
        _inproj_kernel,
        grid=(B, T // tm),
        in_specs=[pl.BlockSpec((1, tm, D), lambda b, i: (b, i, 0)),
                  _const_spec((1, D)), _const_spec(w_all.shape)],
        out_specs=out_specs,
        out_shape=out_shape,
        compiler_params=_cparams("parallel", "parallel"),
    )(x, pre_g.reshape(1, D), w_all)


def _compress_kernel(a_ref, pe_ref, w1_ref, w2_ref, rm_ref, t_ref, hi_scr):
    ncp = a_ref.shape[2]
    a = a_ref[0, 0].astype(F32)
    a_lo = (a + pe_ref[0, 0:1, :]).astype(BF16)
    a_hi = (a + pe_ref[0, 1:2, :]).astype(BF16)
    c = jnp.zeros((ncp, KV_WIDTH), F32)
    hi_scr[pl.ds(ncp, 8), :] = jnp.zeros((8, CMP_HIDDEN), F32)
    for g in range(NSA_KV_HEADS):
        p_lo = _mm(a_lo, w1_ref[0, 2 * g], preferred_element_type=F32)
        hi_scr[pl.ds(0, ncp), :] = _mm(a_hi, w1_ref[0, 2 * g + 1], preferred_element_type=F32)
        hdn = p_lo + hi_scr[pl.ds(1, ncp), :]
        hdn = (hdn * jax.nn.sigmoid(hdn)).astype(BF16)
        c = c + _mm(hdn, w2_ref[0, g], preferred_element_type=F32)
    rows = lax.broadcasted_iota(jnp.int32, c.shape, 0)
    c = jnp.where(rows < ncp - 1, c, 0.0)
    rm_ref[0, 0] = c.astype(BF16)
    for j in range(ncp // CH):
        t_ref[0, 0, j] = c[j * CH:(j + 1) * CH, :].T.astype(BF16)


def _compress_weights(cmp_pe, cmp_w1, cmp_w2):
    half = CMP_LEN // 2
    pe = cmp_pe.reshape(2, 2, half, 1, NSA_HEAD_DIM)
    pe = jnp.broadcast_to(pe, (2, 2, half, NSA_KV_HEADS, NSA_HEAD_DIM)).reshape(2, 2, half * KV_WIDTH)
    w1 = cmp_w1.reshape(2, 2, half, NSA_HEAD_DIM, CMP_HIDDEN)
    w1e = []
    for g in range(NSA_KV_HEADS):
        for hf in range(2):
            z = jnp.zeros((2, half, NSA_KV_HEADS, NSA_HEAD_DIM, CMP_HIDDEN), F32)
            z = z.at[:, :, g].set(w1[:, hf])
            w1e.append(z.reshape(2, half * KV_WIDTH, CMP_HIDDEN))
    w1e = jnp.stack(w1e, axis=1).astype(BF16)
    w2e = []
    for g in range(NSA_KV_HEADS):
        z = jnp.zeros((2, CMP_HIDDEN, NSA_KV_HEADS, NSA_HEAD_DIM), F32)
        z = z.at[:, :, g].set(cmp_w2)
        w2e.append(z.reshape(2, CMP_HIDDEN, KV_WIDTH))
    w2e = jnp.stack(w2e, axis=1).astype(BF16)
    return pe, w1e, w2e


def _compress(cmp, pe, w1e, w2e):
    B, _, T, _ = cmp.shape
    ncp = T // CMP_STRIDE
    a = cmp.reshape(B, 2, ncp, CMP_STRIDE * KV_WIDTH)
    kw = CMP_STRIDE * KV_WIDTH
    return pl.pallas_call(
        _compress_kernel,
        grid=(B, 2),
        in_specs=[pl.BlockSpec((1, 1, ncp, kw), lambda b, k: (b, k, 0, 0)),
                  pl.BlockSpec((1, 2, kw), lambda b, k: (k, 0, 0)),
                  pl.BlockSpec((1, 4, kw, CMP_HIDDEN), lambda b, k: (k, 0, 0, 0)),
                  pl.BlockSpec((1, 2, CMP_HIDDEN, KV_WIDTH), lambda b, k: (k, 0, 0, 0))],
        out_specs=(pl.BlockSpec((1, 1, ncp, KV_WIDTH), lambda b, k: (b, k, 0, 0)),
                   pl.BlockSpec((1, 1, ncp // CH, KV_WIDTH, CH), lambda b, k: (b, k, 0, 0, 0))),
        out_shape=(jax.ShapeDtypeStruct((B, 2, ncp, KV_WIDTH), BF16),
                   jax.ShapeDtypeStruct((B, 2, ncp // CH, KV_WIDTH, CH), BF16)),
        scratch_shapes=[pltpu.VMEM((ncp + 8, CMP_HIDDEN), F32)],
        compiler_params=_cparams("parallel", "parallel"),
    )(a, pe, w1e, w2e)


def _select_kernel(q_ref, kc_ref, vct_ref, ovl_ref, tab_ref, grp_ref, oc_ref, sel_ref, flag_ref,
                   s_scr, p_scr, imp_scr, oc_scr):
    tq = TQ
    nsel = ovl_ref.shape[1]
    q0 = pl.program_id(1) * tq
    qt = q_ref[0, 0]
    span = CH * CMP_STRIDE
    nch = (q0 + tq - CMP_LEN) // span + 1
    nfull = jnp.maximum((q0 - (span + CMP_LEN - CMP_STRIDE - 1)) // span + 1, 0)
    mrel = (lax.broadcasted_iota(jnp.int32, (CH, tq), 0) * CMP_STRIDE + (CMP_LEN - 1)
            - lax.broadcasted_iota(jnp.int32, (CH, tq), 1))
    heads = [slice(hh * tq, (hh + 1) * tq) for hh in range(NSA_HEADS)]

    def chunk_rows(c):
        return pl.ds(pl.multiple_of(c * CH, CH), CH)

    def scores(masked):
        def body(c, ms):
            rows = chunk_rows(c)
            s_c = _mm(kc_ref[0, 0, rows, :], qt, preferred_element_type=F32)
            ok = mrel <= q0 - c * span
            out = []
            for hh, sl in enumerate(heads):
                s = s_c[:, sl] + tab_ref[hh, rows, :]
                if masked:
                    s = jnp.where(ok, s, NEG)
                s_scr[rows, sl] = s
                out.append(jnp.maximum(ms[hh], jnp.max(s, axis=0, keepdims=True)))
            return tuple(out)
        return body

    m0 = tuple(jnp.full((1, tq), NEG, F32) for _ in heads)
    ms = lax.fori_loop(0, nfull, scores(False), m0)
    ms = lax.fori_loop(nfull, nch, scores(True), ms)

    def probs(masked):
        def body(c, dens):
            rows = chunk_rows(c)
            ok = mrel <= q0 - c * span
            out = []
            for hh, sl in enumerate(heads):
                p = jnp.exp2(s_scr[rows, sl] - ms[hh])
                if masked:
                    p = jnp.where(ok, p, 0.0)
                s_scr[rows, sl] = p
                out.append(dens[hh] + jnp.sum(p, axis=0, keepdims=True))
            return tuple(out)
        return body

    d0 = tuple(jnp.zeros((1, tq), F32) for _ in heads)
    dens = lax.fori_loop(0, nfull, probs(False), d0)
    dens = lax.fori_loop(nfull, nch, probs(True), dens)
    rden = [1.0 / jnp.maximum(d, 1e-30) for d in dens]

    imp_scr[...] = jnp.zeros_like(imp_scr)
    oc_scr[...] = jnp.zeros_like(oc_scr)

    def outputs(c, carry):
        rows = chunk_rows(c)
        for g in range(NSA_KV_HEADS):
            psum = jnp.zeros((CH, tq), F32)
            for r in range(NSA_GROUP):
                hh = g * NSA_GROUP + r
                pn = s_scr[rows, heads[hh]] * rden[hh]
                psum = psum + pn
                p_scr[:, heads[hh]] = pn.astype(BF16)
            p_hi = psum.astype(BF16)
            p_lo = (psum - p_hi.astype(F32)).astype(BF16)
            imp_scr[g] += (_mm(ovl_ref[c], p_hi, preferred_element_type=F32)
                           + _mm(ovl_ref[c], p_lo, preferred_element_type=F32))
        oc_scr[...] += _mm(vct_ref[0, 0, c], p_scr[...], preferred_element_type=F32)
        return carry

    lax.fori_loop(0, nch, outputs, 0)
    for hh, sl in enumerate(heads):
        oc_ref[0, 0, hh] = oc_scr[pl.ds((hh // NSA_GROUP) * NSA_HEAD_DIM, NSA_HEAD_DIM), sl]

    n_top = min(SLC_TOPN, nsel)
    n_seen = (q0 + tq - 1) // SLC_LEN + 1
    sizes = sorted({r for r in (nsel * i // 8 for i in range(1, 9)) if r >= n_top and r % 8 == 0})

    def select(rows):
        jb = lax.broadcasted_iota(jnp.int32, (rows, tq), 0)
        cur = (q0 + lax.broadcasted_iota(jnp.int32, (rows, tq), 1)) // SLC_LEN
        forced = (jb == 0) | (jb == cur) | (jb == cur - 1)
        cur1 = (q0 + lax.broadcasted_iota(jnp.int32, (1, tq), 1)) // SLC_LEN
        quota = n_top - (1 + (cur1 >= 1).astype(jnp.int32) + (cur1 >= 2).astype(jnp.int32))
        imps = tuple(jnp.where(forced, TAKEN, jnp.where(jb <= cur, imp_scr[g, pl.ds(0, rows), :], -1.0))
                     for g in range(NSA_KV_HEADS))

        def pick(i, imps):
            out = []
            for imp in imps:
                best = jnp.max(imp, axis=0, keepdims=True)
                first = jnp.min(jnp.where(imp == best, jb, rows), axis=0, keepdims=True)
                first = jnp.where(i < quota, first, -1)
                out.append(jnp.where(jb == first, TAKEN, imp))
            return tuple(out)

        rounds = n_top - jnp.where(q0 >= 2 * SLC_LEN, 3, 1)
        imps = lax.fori_loop(0, rounds, pick, imps)
        ones = jnp.ones((ONES_ROWS, tq), BF16)
        cnt = jnp.zeros((ONES_ROWS, rows), F32)
        for g in range(NSA_KV_HEADS):
            taken = imps[g] < -2.0
            sel_ref[0, 0, g,pl.ds(0, rows), :] = jnp.where(taken, 0.0, NEG)
            if rows < nsel:
                sel_ref[0, 0, g,pl.ds(rows, nsel - rows), :] = jnp.full((nsel - rows, tq), NEG, F32)
            cnt = cnt + _mmg(ones, jnp.where(taken, 1.0, 0.0).astype(BF16), _NT,
                             preferred_element_type=F32)
        act = _mm(jnp.minimum(cnt, 1.0).astype(BF16), grp_ref[pl.ds(0, rows), :],
                  preferred_element_type=F32)
        flag_ref[0, 0] = (act[0:1, :] > 0.0).astype(jnp.int32)

    lo = 0
    for rows in sizes:
        cond = n_seen > lo
        if rows < nsel:
            cond = cond & (n_seen <= rows)
        pl.when(cond)(functools.partial(select, rows))
        lo = rows


def _overlap_t(T):
    ncp = T // CMP_STRIDE
    nc = (T - CMP_LEN) // CMP_STRIDE + 1
    nsel = T // SLC_LEN
    c_start = np.arange(nc) * CMP_STRIDE
    c_end = c_start + CMP_LEN - 1
    s_start = np.arange(nsel) * SLC_LEN
    s_end = s_start + SLC_LEN - 1
    ov = np.clip(np.minimum(c_end[:, None], s_end[None, :])
                 - np.maximum(c_start[:, None], s_start[None, :]) + 1, 0, None) / CMP_LEN
    out = np.zeros((nsel, ncp), np.float32)
    out[:, :nc] = ov.T
    out = out.reshape(nsel, ncp // CH, CH).transpose(1, 0, 2)
    return jnp.asarray(out, dtype=BF16)


def _select_tables(T):
    ncp = T // CMP_STRIDE
    nsel = T // SLC_LEN
    nkt = T // min(TK, T)
    n = np.arange(ncp, dtype=np.float64)[:, None]
    l = np.arange(TQ, dtype=np.float64)[None, :]
    dist = l - (n * CMP_STRIDE + (CMP_LEN - 1) / 2.0)
    tab = np.stack([-a * dist for a in ALIBI2]).astype(np.float32)
    grp = (np.arange(nsel)[:, None] // (nsel // nkt) == np.arange(nkt)[None, :]).astype(np.float32)
    grp[0, :] = 0.0
    return jnp.asarray(tab), jnp.asarray(grp, dtype=BF16)


def _select(q_hm, c_rm, c_t):
    B, T = q_hm.shape[0], q_hm.shape[1] * TQ
    ncp = T // CMP_STRIDE
    nsel = T // SLC_LEN
    nqt = T // TQ
    nkt = T // min(TK, T)
    tab, grp = _select_tables(T)
    return pl.pallas_call(
        _select_kernel,
        grid=(B, nqt),
        in_specs=[pl.BlockSpec((1, 1, LANES, NSA_HEADS * TQ), lambda b, i: (b, i, 0, 0)),
                  pl.BlockSpec((1, 1, ncp, KV_WIDTH), lambda b, i: (b, 0, 0, 0)),
                  pl.BlockSpec((1, 1, ncp // CH, KV_WIDTH, CH), lambda b, i: (b, 1, 0, 0, 0)),
                  _const_spec((ncp // CH, nsel, CH)), _const_spec(tab.shape), _const_spec(grp.shape)],
        out_specs=(pl.BlockSpec((1, 1, NSA_HEADS, NSA_HEAD_DIM, TQ), lambda b, i: (b, i, 0, 0, 0)),
                   pl.BlockSpec((1, 1, NSA_KV_HEADS, nsel, TQ), lambda b, i: (b, i, 0, 0, 0)),
                   pl.BlockSpec((1, 1, 1, nkt), lambda b, i: (b, i, 0, 0))),
        out_shape=(jax.ShapeDtypeStruct((B, nqt, NSA_HEADS, NSA_HEAD_DIM, TQ), F32),
                   jax.ShapeDtypeStruct((B, nqt, NSA_KV_HEADS, nsel, TQ), F32),
                   jax.ShapeDtypeStruct((B, nqt, 1, nkt), jnp.int32)),
        scratch_shapes=[pltpu.VMEM((ncp, NSA_HEADS * TQ), F32),
                        pltpu.VMEM((CH, NSA_HEADS * TQ), BF16),
                        pltpu.VMEM((NSA_KV_HEADS, nsel, TQ), F32),
                        pltpu.VMEM((KV_WIDTH, NSA_HEADS * TQ), F32)],
        compiler_params=_cparams("parallel", "arbitrary"),
    )(q_hm, c_rm, c_t, _overlap_t(T), tab, grp)


def _attn_kernel(flags_ref, q_ref, ks_ref, vst_ref, kw_ref, vwt_ref, sel_ref, oc_ref, gt_ref,
                 gain_ref, kf_ref, kh_ref, qf_ref, wb_ref, o_ref, acc_scr, p_scr, m_scr, alpha_scr,
                 pw_scr, ow_scr, qf_scr, act_scr):
    tq = TQ
    T = ks_ref.shape[1]
    qi = pl.program_id(1)
    q0 = qi * tq
    tk = kf_ref.shape[0]
    wk = wb_ref.shape[1]
    nkt = T // tk
    bpt = tk // SLC_LEN
    vpt = tk // LANES
    assert bpt == 8
    qt = q_ref[0, 0]
    heads = [slice(hh * tq, (hh + 1) * tq) for hh in range(NSA_HEADS)]

    qf_scr[pl.ds(0, LANES), :] = qt
    qf_scr[pl.ds(LANES, LANES), :] = qf_ref[...]
    row8 = lax.broadcasted_iota(jnp.int32, (8, tq), 0)

    def digits3(v):
        d1 = v.astype(BF16).astype(F32)
        d2 = (v - d1).astype(BF16).astype(F32)
        return d1, d2, v - d1 - d2

    def tile(kt, rows, first, head=False):
        k0 = kt * tk
        shift = (q0 - k0).astype(F32)
        for g in range(NSA_KV_HEADS):
            sb = sel_ref[0, 0, g,pl.ds(pl.multiple_of(kt * bpt, bpt), bpt), :]
            for r in range(NSA_GROUP):
                hh = g * NSA_GROUP + r
                d = digits3(jnp.full((8, tq), -ALIBI2[hh], F32) * shift)
                low = jnp.where(row8 == 0, d[0], jnp.where(row8 == 1, d[1], jnp.where(row8 == 2, d[2], 0.0)))
                if head:
                    e = digits3(jnp.full((8, tq), -ALIBI2[hh], F32) * q0.astype(F32))
                    low = jnp.where(row8 == 3, e[0], jnp.where(row8 == 4, e[1], jnp.where(row8 == 5, e[2], low)))
                qf_scr[pl.ds(LANES, 16), heads[hh]] = jnp.concatenate([sb, low], axis=0).astype(BF16)
                if head:
                    qf_scr[pl.ds(LANES + 16, 16), heads[hh]] = jnp.concatenate(
                        [sel_ref[0, 0, g,pl.ds(0, 8), :], jnp.zeros((8, tq), F32)], axis=0).astype(BF16)
        lhs = jnp.concatenate([ks_ref[0, pl.ds(pl.multiple_of(k0, tk), rows), :],
                               kf_ref[pl.ds(0, rows), :]], axis=1)
        vts = [vst_ref[0, kt * vpt + j] for j in range(rows // LANES)]
        if head:
            lhs = jnp.concatenate(
                [jnp.concatenate([ks_ref[0, pl.ds(0, LANES), :], kh_ref[...]], axis=1), lhs], axis=0)
            vts = [vst_ref[0, 0]] + vts
            rows = rows + LANES
        vt = jnp.concatenate(vts, axis=1)
        if first:
            c_i = lax.broadcasted_iota(jnp.int32, (rows, tq), 0)
            l_i = lax.broadcasted_iota(jnp.int32, (rows, tq), 1)
            if head:
                causal = (c_i < LANES) | (c_i + (k0 - q0 - LANES) <= l_i)
            else:
                causal = c_i + (k0 - q0) <= l_i
        s_all = _mm(lhs, qf_scr[...], preferred_element_type=F32)
        for hh, sl in enumerate(heads):
            s = s_all[:, sl]
            if first:
                s = jnp.where(causal, s, NEG)
                m_new = jnp.max(s, axis=0, keepdims=True)
            else:
                m_old = m_scr[:, sl]
                m_new = jnp.maximum(m_old, jnp.max(s, axis=0, keepdims=True))
                alpha_scr[:, sl] = jnp.exp2(m_old - m_new)
            m_scr[:, sl] = m_new
            p_scr[pl.ds(0, rows), sl] = jnp.exp2(s - m_new).astype(BF16)
        pv = _mm(vt, p_scr[pl.ds(0, rows), :], preferred_element_type=F32)
        if first:
            acc_scr[...] = pv
        else:
            acc_scr[...] = acc_scr[...] * alpha_scr[...] + pv

    kt_diag = q0 // tk
    fbase = (pl.program_id(0) * pl.num_programs(1) + qi) * nkt
    with_head = (kt_diag > 0) & (flags_ref[fbase] == 0)
    for jj in range(tk // tq):
        here = qi % (tk // tq) == jj
        pl.when(here & with_head)(functools.partial(tile, kt_diag, (jj + 1) * tq, True, True))
        pl.when(here & jnp.logical_not(with_head))(
            functools.partial(tile, kt_diag, (jj + 1) * tq, True, False))

    def scan(kt, n):
        act_scr[n] = kt
        return n + (flags_ref[fbase + kt] != 0).astype(jnp.int32)

    n_act = lax.fori_loop(0, kt_diag, scan, 0)

    def body(i, carry):
        tile(act_scr[i], tk, False)
        return carry

    lax.fori_loop(0, n_act, body, 0)

    def window(ws, general):
        sw_all = _mm(kw_ref[0, pl.ds(pl.multiple_of(ws, LANES), wk), :], qt,
                     preferred_element_type=F32)
        vwt = jnp.concatenate([vwt_ref[0, ws // LANES + j] for j in range(wk // LANES)], axis=1)
        if general:
            kpos = ws + lax.broadcasted_iota(jnp.int32, (wk, tq), 0)
            dist = q0 + lax.broadcasted_iota(jnp.int32, (wk, tq), 1) - kpos
            wmask = (dist >= 0) & (dist < WINDOW)
            distf = dist.astype(F32)
        for hh, sl in enumerate(heads):
            if general:
                s = jnp.where(wmask, sw_all[:, sl] - ALIBI2[hh] * distf, NEG)
            else:
                s = sw_all[:, sl] + wb_ref[hh]
            pw_scr[:, sl] = jnp.exp2(s - jnp.max(s, axis=0, keepdims=True)).astype(BF16)
        ow_scr[...] = _mm(vwt, pw_scr[...], preferred_element_type=F32)

    @pl.when(q0 >= WINDOW)
    def _():
        window(q0 - WINDOW, False)

    @pl.when(q0 < WINDOW)
    def _():
        window(q0 * 0, True)

    gs = jax.nn.sigmoid(gt_ref[0, 0])
    outs = []
    ssq = jnp.zeros((1, tq), F32)
    for hh in range(NSA_HEADS):
        sl = heads[hh]
        ch = pl.ds((hh // NSA_GROUP) * NSA_HEAD_DIM, NSA_HEAD_DIM)
        o_s = acc_scr[ch, sl] / jnp.maximum(acc_scr[KV_WIDTH:KV_WIDTH + 1, sl], 1e-30)
        o_w = ow_scr[ch, sl] / jnp.maximum(ow_scr[KV_WIDTH:KV_WIDTH + 1, sl], 1e-30)
        o = (gs[3 * hh:3 * hh + 1, :] * oc_ref[0, 0, hh] + gs[3 * hh + 1:3 * hh + 2, :] * o_s
             + gs[3 * hh + 2:3 * hh + 3, :] * o_w)
        ssq = ssq + jnp.sum(o * o, axis=0, keepdims=True)
        outs.append(o)
    rinv = lax.rsqrt(ssq / NSA_WIDTH + EPS)
    for j in range(NSA_HEADS // 2):
        pair = jnp.concatenate([outs[2 * j] * rinv * gain_ref[2 * j],
                                outs[2 * j + 1] * rinv * gain_ref[2 * j + 1]], axis=0)
        o_ref[0, :, j * LANES:(j + 1) * LANES] = pair.T.astype(BF16)


def _attn_gain(nsa_out_g):
    g = nsa_out_g.reshape(NSA_HEADS, NSA_HEAD_DIM)
    return jnp.broadcast_to(g[:, :, None], (NSA_HEADS, NSA_HEAD_DIM, TQ))


def _bf16_digits(x, n):
    out = []
    for _ in range(n):
        d = float(np.asarray(x, np.float32).astype(jnp.bfloat16).astype(np.float32))
        out.append(d)
        x = x - d
    return out


def _attn_tables(T):
    tk = min(TK, T)
    wk = min(WKEYS, T)
    c = np.arange(tk)
    kf = np.zeros((tk, LANES), np.float32)
    kf[c, c // SLC_LEN] = 1.0
    kf[:, 8:11] = 1.0
    kf[:, 32:35] = (c // 32)[:, None]
    kf[:, 35:38] = (c % 32)[:, None]
    ch = np.arange(LANES)
    kh = np.zeros((LANES, LANES), np.float32)
    kh[ch, 16 + ch // SLC_LEN] = 1.0
    kh[:, 11:14] = 1.0
    kh[:, 32:35] = (ch // 32)[:, None]
    kh[:, 35:38] = (ch % 32)[:, None]
    qf = np.zeros((LANES, NSA_HEADS * TQ), np.float32)
    for hh, a in enumerate(ALIBI2):
        d = _bf16_digits(a, 3)
        for i in range(3):
            qf[32 + i, hh * TQ:(hh + 1) * TQ] = 32.0 * d[i]
            qf[35 + i, hh * TQ:(hh + 1) * TQ] = d[i]
    l = np.arange(TQ, dtype=np.float64)[None, :]
    dist = WINDOW + l - np.arange(wk, dtype=np.float64)[:, None]
    inside = (dist >= 0) & (dist < WINDOW)
    wb = np.stack([np.where(inside, -a * dist, NEG) for a in ALIBI2]).astype(np.float32)
    return (jnp.asarray(kf, dtype=BF16), jnp.asarray(kh, dtype=BF16), jnp.asarray(qf, dtype=BF16),
            jnp.asarray(wb))


def _attn(q_hm, ks, vst, kw, vwt, sel, flags, oc, gt, gain_b):
    B, T = q_hm.shape[0], q_hm.shape[1] * TQ
    nsel = T // SLC_LEN
    nqt = T // TQ
    kf, kh, qf, wb = _attn_tables(T)
    tk, wk = kf.shape[0], wb.shape[1]
    once = dict(pipeline_mode=pl.Buffered(1))
    full_k = pl.BlockSpec((1, T, KV_WIDTH), lambda b, i, f: (b, 0, 0), **once)
    full_vt = pl.BlockSpec((1, T // LANES, VT_ROWS, LANES), lambda b, i, f: (b, 0, 0, 0), **once)
    const = lambda shape: pl.BlockSpec(shape, lambda b, i, f: (0,) * len(shape), **once)
    grid_spec = pltpu.PrefetchScalarGridSpec(
        num_scalar_prefetch=1,
        grid=(B, nqt),
        in_specs=[pl.BlockSpec((1, 1, LANES, NSA_HEADS * TQ), lambda b, i, f: (b, i, 0, 0)),
                  full_k, full_vt, full_k, full_vt,
                  pl.BlockSpec((1, 1, NSA_KV_HEADS, nsel, TQ), lambda b, i, f: (b, i, 0, 0, 0)),
                  pl.BlockSpec((1, 1, NSA_HEADS, NSA_HEAD_DIM, TQ), lambda b, i, f: (b, i, 0, 0, 0)),
                  pl.BlockSpec((1, 1, GATE_ROWS, TQ), lambda b, i, f: (b, i, 0, 0)),
                  const((NSA_HEADS, NSA_HEAD_DIM, TQ)), const(kf.shape), const(kh.shape),
                  const(qf.shape), const(wb.shape)],
        out_specs=pl.BlockSpec((1, TQ, NSA_WIDTH), lambda b, i, f: (b, i, 0)),
        scratch_shapes=[pltpu.VMEM((VT_ROWS, NSA_HEADS * TQ), F32),
                        pltpu.VMEM((tk + LANES, NSA_HEADS * TQ), BF16),
                        pltpu.VMEM((1, NSA_HEADS * TQ), F32),
                        pltpu.VMEM((1, NSA_HEADS * TQ), F32),
                        pltpu.VMEM((wk, NSA_HEADS * TQ), BF16),
                        pltpu.VMEM((VT_ROWS, NSA_HEADS * TQ), F32),
                        pltpu.VMEM((2 * LANES, NSA_HEADS * TQ), BF16),
                        pltpu.SMEM((T // tk,), jnp.int32)])
    return pl.pallas_call(
        _attn_kernel,
        grid_spec=grid_spec,
        out_shape=jax.ShapeDtypeStruct((B, T, NSA_WIDTH), BF16),
        compiler_params=_cparams("parallel", "arbitrary"),
    )(flags.reshape(-1), q_hm, ks, vst, kw, vwt, sel, oc, gt, gain_b, kf, kh, qf, wb)


def _retention_kernel(q_ref, k_ref, v_ref, g_ref, dm_ref, qd_ref, kd_ref, gn_ref, o_ref, st_scr,
                      *, chunk_dec):
    rt = q_ref.shape[1]
    C, d = RET_CHUNK, RET_HEAD_DIM

    @pl.when(pl.program_id(1) == 0)
    def _():
        st_scr[...] = jnp.zeros_like(st_scr)

    for c in range(rt // C):
        rows = slice(c * C, (c + 1) * C)
        for h in range(RET_HEADS):
            cols = slice(h * d, (h + 1) * d)
            q = q_ref[0, rows, cols]
            k = k_ref[0, rows, cols]
            v = v_ref[0, rows, cols]
            state = st_scr[h]
            s = _mmg(q, k, _NT, preferred_element_type=F32) * dm_ref[h]
            inner = _mm(s.astype(BF16), v, preferred_element_type=F32)
            cross = _mm(q, state.astype(BF16), preferred_element_type=F32) * qd_ref[h]
            kd = (k.astype(F32) * kd_ref[h]).astype(BF16)
            st_scr[h] = state * chunk_dec[h] + _mmg(kd, v, _TN, preferred_element_type=F32)
            o = inner + cross
            mu = jnp.mean(o, axis=-1, keepdims=True)
            oc = o - mu
            var = jnp.mean(oc * oc, axis=-1, keepdims=True)
            o = oc * lax.rsqrt(var + EPS) * gn_ref[:, cols]
            gg = g_ref[0, rows, cols]
            o_ref[0, rows, cols] = (o * (gg * jax.nn.sigmoid(gg))).astype(BF16)


def _retention_consts():
    C, d = RET_CHUNK, RET_HEAD_DIM
    log_g = np.log(1.0 - np.exp2(-5.0 - np.arange(RET_HEADS, dtype=np.float64)))
    pos = np.arange(C, dtype=np.float64)
    diff = pos[:, None] - pos[None, :]
    scale = d ** -0.5
    dmask = np.where(diff >= 0, np.exp(np.maximum(diff, 0.0) * log_g[:, None, None]), 0.0) * scale
    q_dec = np.exp((pos + 1.0) * log_g[:, None])
    k_dec = np.exp((C - 1.0 - pos) * log_g[:, None]) * scale
    chunk_dec = tuple(float(x) for x in np.exp(C * log_g))
    bc = lambda a: jnp.asarray(np.broadcast_to(a[:, :, None], (RET_HEADS, C, d)).astype(np.float32))
    return jnp.asarray(dmask.astype(np.float32)), bc(q_dec), bc(k_dec), chunk_dec


def _retention(qr, kr, vr, gr, gn_gain):
    B, T, W = qr.shape
    rt = min(2 * TM, T)
    dmask, qd, kd, chunk_dec = _retention_consts()
    row = pl.BlockSpec((1, rt, W), lambda b, i: (b, i, 0))
    cst = _const_spec((RET_HEADS, RET_CHUNK, RET_HEAD_DIM))
    return pl.pallas_call(
        functools.partial(_retention_kernel, chunk_dec=chunk_dec),
        grid=(B, T // rt),
        in_specs=[row, row, row, row, _const_spec(dmask.shape), cst, cst, _const_spec((1, W))],
        out_specs=row,
        out_shape=jax.ShapeDtypeStruct((B, T, W), BF16),
        scratch_shapes=[pltpu.VMEM((RET_HEADS, RET_HEAD_DIM, RET_HEAD_DIM), F32)],
        compiler_params=_cparams("parallel", "arbitrary"),
    )(qr, kr, vr, gr, dmask, qd, kd, gn_gain.reshape(1, W))


def _outproj_kernel(x_ref, on_ref, or_ref, wn_ref, wr_ref, post_ref, o_ref):
    y = (_mm(on_ref[0], wn_ref[...], preferred_element_type=F32)
         + _mm(or_ref[0], wr_ref[...], preferred_element_type=F32))
    o_ref[0] = x_ref[0] + _rms(y, post_ref[...])


def _outproj_weights(w_out):
    return w_out[:NSA_WIDTH].astype(BF16), w_out[NSA_WIDTH:].astype(BF16)


def _outproj(x, o_nsa, o_ret, wn, wr, post_g):
    B, T, D = x.shape
    tm = min(2 * TM, T)
    row = lambda w: pl.BlockSpec((1, tm, w), lambda b, i: (b, i, 0))
    return pl.pallas_call(
        _outproj_kernel,
        grid=(B, T // tm),
        in_specs=[row(D), row(NSA_WIDTH), row(RET_WIDTH), _const_spec(wn.shape), _const_spec(wr.shape),
                  _const_spec((1, D))],
        out_specs=row(D),
        out_shape=jax.ShapeDtypeStruct(x.shape, F32),
        compiler_params=_cparams("parallel", "parallel"),
    )(x, o_nsa, o_ret, wn, wr, post_g.reshape(1, D))


def _memkv_kernel(m_ref, g_ref, wk_ref, wv_ref, k_ref, v_ref):
    h = _rms(m_ref[0], g_ref[...]).astype(BF16)
    k_ref[0] = _mm(h, wk_ref[...], preferred_element_type=F32).astype(BF16)
    v_ref[0] = _mm(h, wv_ref[...], preferred_element_type=F32).astype(BF16)


def _memkv(mem, g, wk, wv):
    B, M, D = mem.shape
    blk = pl.BlockSpec((1, M, D), lambda b: (b, 0, 0))
    return pl.pallas_call(
        _memkv_kernel,
        grid=(B,),
        in_specs=[blk, _const_spec((1, D)), _const_spec(wk.shape), _const_spec(wv.shape)],
        out_specs=(blk, blk),
        out_shape=(jax.ShapeDtypeStruct(mem.shape, BF16), jax.ShapeDtypeStruct(mem.shape, BF16)),
        compiler_params=_cparams("parallel"),
    )(mem, g.reshape(1, D), wk, wv)


def _xattn_kernel(x_ref, pre_ref, wq_ref, k_ref, v_ref, wo_ref, post_ref, o_ref):
    x = x_ref[0]
    h = _rms(x, pre_ref[...]).astype(BF16)
    q = (_mm(h, wq_ref[...], preferred_element_type=F32) * (MEM_HEAD_DIM ** -0.5)).astype(BF16)
    heads = []
    for hd in range(MEM_HEADS):
        cols = slice(hd * MEM_HEAD_DIM, (hd + 1) * MEM_HEAD_DIM)
        s = _mmg(q[:, cols], k_ref[0, :, cols], _NT, preferred_element_type=F32)
        p = jnp.exp(s - jnp.max(s, axis=-1, keepdims=True))
        den = jnp.sum(p, axis=-1, keepdims=True)
        oh = _mm(p.astype(BF16), v_ref[0, :, cols], preferred_element_type=F32) / den
        heads.append(oh.astype(BF16))
    o = jnp.concatenate(heads, axis=1)
    y = _mm(o, wo_ref[...], preferred_element_type=F32)
    o_ref[0] = x + _rms(y, post_ref[...])


def _xattn(x, pre_g, wq, km, vm, wo, post_g):
    B, T, D = x.shape
    M = km.shape[1]
    tm = min(2 * TM, T)
    row = pl.BlockSpec((1, tm, D), lambda b, i: (b, i, 0))
    kv = pl.BlockSpec((1, M, D), lambda b, i: (b, 0, 0))
    return pl.pallas_call(
        _xattn_kernel,
        grid=(B, T // tm),
        in_specs=[row, _const_spec((1, D)), _const_spec(wq.shape), kv, kv, _const_spec(wo.shape),
                  _const_spec((1, D))],
        out_specs=row,
        out_shape=jax.ShapeDtypeStruct(x.shape, F32),
        compiler_params=_cparams("parallel", "parallel"),
    )(x, pre_g.reshape(1, D), wq, km, vm, wo, post_g.reshape(1, D))


def _hybrid_mixer(x, pre_g, w_in, cmp_pe, cmp_w1, cmp_w2, nsa_out_g, ret_gn_g, w_out, post_g):
    q_hm, cmp, ks, vst, kw, vwt, gt, qr, kr, vr, gr = _inproj(x, pre_g, _inproj_weight(w_in))
    c_rm, c_t = _compress(cmp, *_compress_weights(cmp_pe, cmp_w1, cmp_w2))
    oc, sel, flags = _select(q_hm, c_rm, c_t)
    o_nsa = _attn(q_hm, ks, vst, kw, vwt, sel, flags, oc, gt, _attn_gain(nsa_out_g))
    o_ret = _retention(qr, kr, vr, gr, ret_gn_g)
    wn, wr = _outproj_weights(w_out)
    return _outproj(x, o_nsa, o_ret, wn, wr, post_g)


def kernel(x, mem, ffn1_pre_g, ffn1_w_gate, ffn1_w_up, ffn1_w_down, ffn1_post_g, mix_pre_g, w_in, cmp_pe, cmp_w1, cmp_w2, nsa_out_g, ret_gn_g, w_out, mix_post_g, xa_pre_g, xa_mem_g, xa_wq, xa_wk, xa_wv, xa_wo, xa_post_g, ffn2_pre_g, ffn2_w_gate, ffn2_w_up, ffn2_w_down, ffn2_post_g):
    depth = w_in.shape[0]
    bf = lambda w: w.astype(BF16)
    for l in range(depth):
        x = _ffn(x, ffn1_pre_g[l], bf(ffn1_w_gate[l]), bf(ffn1_w_up[l]), bf(ffn1_w_down[l]),
                 ffn1_post_g[l])
        x = _hybrid_mixer(x, mix_pre_g[l], w_in[l], cmp_pe[l], cmp_w1[l], cmp_w2[l], nsa_out_g[l],
                          ret_gn_g[l], w_out[l], mix_post_g[l])
        km, vm = _memkv(mem, xa_mem_g[l], bf(xa_wk[l]), bf(xa_wv[l]))
        x = _xattn(x, xa_pre_g[l], bf(xa_wq[l]), km, vm, bf(xa_wo[l]), xa_post_g[l])
        x = _ffn(x, ffn2_pre_g[l], bf(ffn2_w_gate[l]), bf(ffn2_w_up[l]), bf(ffn2_w_down[l]),
                 ffn2_post_g[l])
    return x
```
